```python
import jax, jax.numpy as jnp
from jax import lax
import numpy as np


D_MODEL = 2048
BATCH = 4
SEQ = 4096
DEPTH = 1

N_META = 16
BLOCK = 128
NORM_EPS = 1e-6

RWKV_HEADS = 16
RWKV_HEAD = 64
RWKV_DIM = RWKV_HEADS * RWKV_HEAD
DECAY_LORA = 64
ICLR_LORA = 64
GATE_LORA = 160
GN_EPS = 64e-5

MLA_HEADS = 16
Q_LORA = 512
KV_LORA = 512
QK_NOPE = 128
QK_ROPE = 64
V_HEAD = 128
MLA_DIM = MLA_HEADS * V_HEAD
ROPE_THETA = 10000.0

N_KEYS = 128
N_EXPERTS = N_KEYS * N_KEYS
PEER_HEADS = 8
PEER_QDIM = 256
PEER_HALF = PEER_QDIM // 2
PEER_TOPK = 16

RWKV_SPLIT = (RWKV_DIM, RWKV_DIM, RWKV_DIM, GATE_LORA, DECAY_LORA, DECAY_LORA, ICLR_LORA, ICLR_LORA)
RWKV_COLS = sum(RWKV_SPLIT)
IN_SPLIT = (RWKV_COLS, Q_LORA, KV_LORA, QK_ROPE, D_MODEL, D_MODEL)
IN_COLS = sum(IN_SPLIT)

kernel_name = 'hybrid_rwkv7_mla_peer_encoder_block'


def split_cols(t, sizes):
    return jnp.split(t, np.cumsum(sizes)[:-1].tolist(), axis=-1)


def rmsnorm(x, g):
    xf = x.astype(jnp.float32)
    y = xf * lax.rsqrt(jnp.mean(xf * xf, axis=-1, keepdims=True) + NORM_EPS)
    return (y * g.astype(jnp.float32)).astype(x.dtype)


def centred_shift(p, c):
    prev = jnp.pad(p, ((0, 0), (1, 0), (0, 0)))[:, :-1]
    nxt = jnp.pad(p, ((0, 0), (0, 1), (0, 0)))[:, 1:]
    return c[0] * prev + c[1] * p + c[2] * nxt


def wkv7_scan(r, w, k, v, a, b):
    Bsz, L, H, N = r.shape

    def step(S, inp):
        r_t, w_t, k_t, v_t, a_t, b_t = inp
        sa = jnp.einsum('bhij,bhj->bhi', S, a_t)
        S = S * w_t[:, :, None, :] + sa[..., None] * b_t[:, :, None, :] + v_t[..., None] * k_t[:, :, None, :]
        return S, jnp.einsum('bhij,bhj->bhi', S, r_t)

    xs = tuple(jnp.moveaxis(t, 1, 0) for t in (r, w, k, v, a, b))
    S0 = jnp.zeros((Bsz, H, N, N), jnp.float32)
    _, y = lax.scan(step, S0, xs)
    return jnp.moveaxis(y, 0, 1)


def rwkv7_branch(cols, shift_c, w_up, w0, a_up, a0, g_up, k_k, k_a, r_k, ln_g, ln_b):
    f32 = jnp.float32
    Bsz, L, _ = cols.shape
    xs = centred_shift(cols.astype(f32), shift_c.astype(f32))
    r, k, v, g_d, wd_f, wd_b, ad_f, ad_b = split_cols(xs, RWKV_SPLIT)
    heads = lambda t: t.reshape(t.shape[:-1] + (RWKV_HEADS, RWKV_HEAD))
    g = jax.nn.sigmoid(g_d) @ g_up.astype(f32)
    kk = heads(k * k_k.astype(f32))
    kk = kk / jnp.maximum(jnp.linalg.norm(kk, axis=-1, keepdims=True), 1e-12)
    rh, kh, vh = heads(r), heads(k), heads(v)

    def direction(wd, ad, d, reverse):
        w_log = -jax.nn.softplus(-(w0[d].astype(f32) + jnp.tanh(wd) @ w_up[d].astype(f32))) - 0.5
        decay = jnp.exp(-jnp.exp(heads(w_log)))
        iclr = jax.nn.sigmoid(heads(a0[d].astype(f32) + ad @ a_up[d].astype(f32)))
        kt = kh * (1.0 + (iclr - 1.0) * heads(k_a.astype(f32)))
        ins = (rh, decay, kt, vh, -kk, kk * iclr)
        if reverse:
            ins = tuple(jnp.flip(t, 1) for t in ins)
        y = wkv7_scan(*ins)
        if reverse:
            y = jnp.flip(y, 1)
        return y, kt

    y_f, kt_f = direction(wd_f, ad_f, 0, False)
    y_b, kt_b = direction(wd_b, ad_b, 1, True)
    y = y_f + y_b
    mu = jnp.mean(y, axis=-1, keepdims=True)
    var = jnp.mean(jnp.square(y - mu), axis=-1, keepdims=True)
    y = ((y - mu) * lax.rsqrt(var + GN_EPS)).reshape(Bsz, L, RWKV_DIM) * ln_g.astype(f32) + ln_b.astype(f32)
    bonus = jnp.sum(rh * (0.5 * (kt_f + kt_b)) * r_k.astype(f32), axis=-1, keepdims=True) * vh
    return ((y + bonus.reshape(Bsz, L, RWKV_DIM)) * g).astype(cols.dtype)


def apply_rope(x, cos, sin):
    x1, x2 = jnp.split(x, 2, axis=-1)
    return jnp.concatenate([x1 * cos - x2 * sin, x1 * sin + x2 * cos], axis=-1)


def mla_branch(q_d, kv_d, k_rope_raw, q_norm_g, w_uq, kv_norm_g, w_ukv, cos, sin):
    Bsz, L, _ = q_d.shape
    q = (rmsnorm(q_d, q_norm_g) @ w_uq).reshape(Bsz, L, MLA_HEADS, QK_NOPE + QK_ROPE)
    q_nope = q[..., :QK_NOPE]
    q_rope = apply_rope(q[..., QK_NOPE:], cos[:, None, :], sin[:, None, :])
    kv = (rmsnorm(kv_d, kv_norm_g) @ w_ukv).reshape(Bsz, L, MLA_HEADS, QK_NOPE + V_HEAD)
    k_nope, v = kv[..., :QK_NOPE], kv[..., QK_NOPE:]
    k_rope = apply_rope(k_rope_raw, cos, sin)
    pad = (-N_META) % BLOCK
    Lq = L + pad
    nblk = Lq // BLOCK
    to_blocks = lambda t: jnp.pad(t, ((0, 0), (pad, 0), (0, 0), (0, 0))).reshape(
        Bsz, nblk, BLOCK, MLA_HEADS, t.shape[-1]).transpose(1, 0, 2, 3, 4)
    qn_blk, qr_blk = to_blocks(q_nope), to_blocks(q_rope)
    scale = (QK_NOPE + QK_ROPE) ** -0.5

    def attend(blk):
        qn_b, qr_b = blk
        s = jnp.einsum('bqhd,bkhd->bhqk', qn_b, k_nope) + jnp.einsum('bqhd,bkd->bhqk', qr_b, k_rope)
        p = jax.nn.softmax(s.astype(jnp.float32) * scale, axis=-1).astype(v.dtype)
        return jnp.einsum('bhqk,bkhd->bqhd', p, v)

    o = lax.map(attend, (qn_blk, qr_blk))
    return o.transpose(1, 0, 2, 3, 4).reshape(Bsz, Lq, MLA_DIM)[:, pad:]


def peer_ffn(h, w_q, sub_keys, u_tab, v_tab):
    Bsz, L, D = h.shape
    q = (h @ w_q).reshape(Bsz, L, PEER_HEADS, 2, PEER_HALF)
    s = jnp.einsum('blhpd,hpnd->blhpn', q, sub_keys)
    top_s, top_i = lax.top_k(s, PEER_TOPK)
    cand = top_s[..., 0, :, None] + top_s[..., 1, None, :]
    best_s, best_c = lax.top_k(cand.reshape(Bsz, L, PEER_HEADS, PEER_TOPK * PEER_TOPK), PEER_TOPK)
    i1 = jnp.take_along_axis(top_i[..., 0, :], best_c // PEER_TOPK, axis=-1)
    i2 = jnp.take_along_axis(top_i[..., 1, :], best_c % PEER_TOPK, axis=-1)
    expert = i1 * N_KEYS + i2
    gate = jax.nn.softmax(best_s.astype(jnp.float32), axis=-1).astype(h.dtype)
    pad = (-N_META) % BLOCK
    Lp = L + pad
    nblk = Bsz * Lp // BLOCK
    padt = lambda t: jnp.pad(t, ((0, 0), (pad, 0)) + ((0, 0),) * (t.ndim - 2))
    h_blk = padt(h).reshape(nblk, BLOCK, D)
    e_blk = padt(expert).reshape(nblk, BLOCK, PEER_HEADS, PEER_TOPK)
    g_blk = padt(gate).reshape(nblk, BLOCK, PEER_HEADS, PEER_TOPK)

    def experts(blk):
        h_b, e_b, g_b = blk
        u = jnp.take(u_tab, e_b, axis=0)
        act = jax.nn.gelu(jnp.einsum('td,thkd->thk', h_b, u), approximate=False) * g_b
        return jnp.einsum('thk,thkd->td', act, jnp.take(v_tab, e_b, axis=0))

    out = lax.map(experts, (h_blk, e_blk, g_blk))
    return out.reshape(Bsz, Lp, D)[:, pad:]


def setup_inputs(seed: int = 0) -> dict:
    key = jax.random.key(seed)
    ks = jax.random.split(key, 32)
    nrm = lambda k, shape, scale: jax.random.normal(k, shape, jnp.float32) * scale
    n_idx = jnp.arange(RWKV_DIM, dtype=jnp.float32) / (RWKV_DIM - 1)
    decay_base = -6.5 + 5.0 * n_idx ** 0.85
    shift_base = jnp.array([0.25, 0.5, 0.25], jnp.float32)[:, None]
    return {
        'x': nrm(ks[0], (BATCH, SEQ, D_MODEL), 1.0),
        'meta_tokens': nrm(ks[1], (N_META, D_MODEL), 1.0),
        'norm_mix_g': 1.0 + nrm(ks[2], (DEPTH, D_MODEL), 0.02),
        'w_in': nrm(ks[3], (DEPTH, D_MODEL, IN_COLS), D_MODEL ** -0.5),
        'b_gate': nrm(ks[4], (DEPTH, 2 * D_MODEL), 0.02),
        'shift_c': shift_base + nrm(ks[5], (DEPTH, 3, RWKV_COLS), 0.05),
        'w_up': nrm(ks[6], (DEPTH, 2, DECAY_LORA, RWKV_DIM), 0.1 * DECAY_LORA ** -0.5),
        'w0': decay_base + nrm(ks[7], (DEPTH, 2, RWKV_DIM), 0.1),
        'a_up': nrm(ks[8], (DEPTH, 2, ICLR_LORA, RWKV_DIM), 0.5 * ICLR_LORA ** -0.5),
        'a0': nrm(ks[9], (DEPTH, 2, RWKV_DIM), 0.1),
        'g_up': nrm(ks[10], (DEPTH, GATE_LORA, RWKV_DIM), GATE_LORA ** -0.5),
        'k_k': 0.85 + nrm(ks[11], (DEPTH, RWKV_DIM), 0.02),
        'k_a': 1.0 + nrm(ks[12], (DEPTH, RWKV_DIM), 0.02),
        'r_k': nrm(ks[13], (DEPTH, RWKV_HEADS, RWKV_HEAD), 0.1),
        'ln_x_g': 1.0 + nrm(ks[14], (DEPTH, RWKV_DIM), 0.02),
        'ln_x_b': nrm(ks[15], (DEPTH, RWKV_DIM), 0.02),
        'q_norm_g': 1.0 + nrm(ks[16], (DEPTH, Q_LORA), 0.02),
        'w_uq': nrm(ks[17], (DEPTH, Q_LORA, MLA_HEADS * (QK_NOPE + QK_ROPE)), Q_LORA ** -0.5),
        'kv_norm_g': 1.0 + nrm(ks[18], (DEPTH, KV_LORA), 0.02),
        'w_ukv': nrm(ks[19], (DEPTH, KV_LORA, MLA_HEADS * (QK_NOPE + V_HEAD)), KV_LORA ** -0.5),
        'p_rwkv': nrm(ks[20], (DEPTH, RWKV_DIM, D_MODEL), RWKV_DIM ** -0.5),
        'p_mla': nrm(ks[21], (DEPTH, MLA_DIM, D_MODEL), MLA_DIM ** -0.5),
        'w_o': nrm(ks[22], (DEPTH, D_MODEL, D_MODEL), D_MODEL ** -0.5),
        'norm_ffn_g': 1.0 + nrm(ks[23], (DEPTH, D_MODEL), 0.02),
        'peer_wq': nrm(ks[24], (DEPTH, D_MODEL, PEER_HEADS * PEER_QDIM), D_MODEL ** -0.5),
        'peer_keys': nrm(ks[25], (DEPTH, PEER_HEADS, 2, N_KEYS, PEER_HALF), PEER_HALF ** -0.5),
        'peer_u': nrm(ks[26], (DEPTH, N_EXPERTS, D_MODEL), D_MODEL ** -0.5),
        'peer_v': nrm(ks[27], (DEPTH, N_EXPERTS, D_MODEL), PEER_HEADS ** -0.5),
        'final_norm_g': 1.0 + nrm(ks[28], (D_MODEL,), 0.02),
    }


def reference(x, meta_tokens, norm_mix_g, w_in, b_gate, shift_c, w_up, w0, a_up, a0, g_up, k_k, k_a, r_k,
              ln_x_g, ln_x_b, q_norm_g, w_uq, kv_norm_g, w_ukv, p_rwkv, p_mla, w_o, norm_ffn_g,
              peer_wq, peer_keys, peer_u, peer_v, final_norm_g):
    Bsz = x.shape[0]
    meta = jnp.broadcast_to(meta_tokens.astype(x.dtype)[None], (Bsz, N_META, D_MODEL))
    h = jnp.concatenate([meta, x], axis=1)
    L = h.shape[1]
    pos = jnp.arange(L, dtype=jnp.float32)
    inv_freq = ROPE_THETA ** (-jnp.arange(0, QK_ROPE, 2, dtype=jnp.float32) / QK_ROPE)
    ang = pos[:, None] * inv_freq[None, :]
    cos, sin = jnp.cos(ang).astype(x.dtype), jnp.sin(ang).astype(x.dtype)
    for l in range(DEPTH):
        n = rmsnorm(h, norm_mix_g[l])
        proj = n @ w_in[l]
        rw_cols, q_d, kv_d, k_rope_raw, gr_pre, gm_pre = split_cols(proj, IN_SPLIT)
        y_r = rwkv7_branch(rw_cols, shift_c[l], w_up[l], w0[l], a_up[l], a0[l], g_up[l],
                           k_k[l], k_a[l], r_k[l], ln_x_g[l], ln_x_b[l])
        y_m = mla_branch(q_d, kv_d, k_rope_raw, q_norm_g[l], w_uq[l], kv_norm_g[l], w_ukv[l], cos, sin)
        gates = jax.nn.sigmoid(jnp.concatenate([gr_pre, gm_pre], axis=-1) + b_gate[l])
        g_r, g_m = jnp.split(gates, 2, axis=-1)
        merged = g_r * (y_r @ p_rwkv[l]) + g_m * (y_m @ p_mla[l])
        h = h + merged @ w_o[l]
        h = h + peer_ffn(rmsnorm(h, norm_ffn_g[l]), peer_wq[l], peer_keys[l], peer_u[l], peer_v[l])
    return rmsnorm(h, final_norm_g)[:, N_META:]
```

```python
import functools
import math

import jax
import jax.numpy as jnp
import numpy as np
from jax import lax
from jax.experimental import pallas as pl
from jax.experimental.pallas import tpu as pltpu

F32 = jnp.float32
BF16 = jnp.bfloat16

N_META = 16
PAD_ROWS = 112
TAIL = PAD_ROWS + N_META
NORM_EPS = 1e-6
GN_EPS = 64e-5
ROPE_THETA = 10000.0

RWKV_HEADS = 16
RWKV_HEAD = 64
RWKV_DIM = RWKV_HEADS * RWKV_HEAD
GATE_LORA = 160
LORA = 64
CHUNK = 64

MLA_HEADS = 16
QK_NOPE = 128
QK_ROPE = 64
V_HEAD = 128
QK_PAD = 256

N_KEYS = 128
PEER_HEADS = 8
PEER_TOPK = 16
PEER_SEL = PEER_HEADS * PEER_TOPK
PEER_TOK = 8

VMEM_LIMIT = 56 * 1024 * 1024


def _cp(sem, vmem=VMEM_LIMIT):
    return pltpu.CompilerParams(dimension_semantics=sem, vmem_limit_bytes=vmem)


def _pick(n, cands):
    for c in cands:
        if n % c == 0:
            return c
    raise ValueError(f"no tile for {n}")


def _dot(a, b):
    return jnp.dot(a, b, preferred_element_type=F32)


def _dot_nt(a, b):
    return lax.dot_general(a, b, (((1,), (1,)), ((), ())), preferred_element_type=F32)


def _split_bf16(x):
    hi = x.astype(BF16)
    lo = (x - hi.astype(F32)).astype(BF16)
    return hi, lo


def _dot2(x, w):
    hi, lo = _split_bf16(x)
    return _dot(hi, w) + _dot(lo, w)


def _pack_kernel(u_ref, v_ref, o_ref):
    ub = pltpu.bitcast(u_ref[...].astype(BF16).astype(F32), jnp.uint32)
    vb = pltpu.bitcast(v_ref[...].astype(BF16).astype(F32), jnp.uint32)
    o_ref[...] = (ub >> 16) | (vb & jnp.uint32(0xFFFF0000))


def _pack_tables(u, v):
    n, d = u.shape
    tr = _pick(n, (512, 256, 128, 8))
    spec = pl.BlockSpec((tr, d), lambda i: (i, 0))
    return pl.pallas_call(
        _pack_kernel, grid=(n // tr,), in_specs=[spec, spec], out_specs=spec,
        out_shape=jax.ShapeDtypeStruct((n, d), jnp.uint32),
        compiler_params=_cp(("parallel",)), name="peer_pack")(u, v)


def _norm_rows(x, g):
    ms = jnp.mean(x * x, axis=-1, keepdims=True)
    return x * lax.rsqrt(ms + NORM_EPS) * g


def _norm_mm_kernel(x_ref, g_ref, w_ref, o_ref, nb_ref):
    @pl.when(pl.program_id(2) == 0)
    def _():
        nb_ref[...] = _norm_rows(x_ref[...], g_ref[...]).astype(BF16)

    o_ref[...] = _dot(nb_ref[...], w_ref[...]).astype(o_ref.dtype)


def _norm_mm(x, g, w, tm, tn, out_dtype=F32, name="norm_mm"):
    b, l, k = x.shape
    n = w.shape[1]
    return pl.pallas_call(
        _norm_mm_kernel, grid=(b, l // tm, n // tn),
        in_specs=[pl.BlockSpec((None, tm, k), lambda bi, i, j: (bi, i, 0)),
                  pl.BlockSpec((1, k), lambda bi, i, j: (0, 0)),
                  pl.BlockSpec((k, tn), lambda bi, i, j: (0, j))],
        out_specs=pl.BlockSpec((None, tm, tn), lambda bi, i, j: (bi, i, j)),
        out_shape=jax.ShapeDtypeStruct((b, l, n), out_dtype),
        scratch_shapes=[pltpu.VMEM((tm, k), BF16)],
        compiler_params=_cp(("parallel", "parallel", "arbitrary")), name=name)(x, g, w)


def _rwkv_prep_kernel(p_ref, pp_ref, pn_ref, sc_ref, wl_ref, w0_ref, a0_ref, gup_ref, kkw_ref, ka_ref, rk_ref,
                      ones_ref, r_ref, v_ref, kk_ref, lwf_ref, lwb_ref, ktf_ref, ktb_ref, bf_ref, bb_ref,
                      g_ref, bonus_ref, *, seq, tm):
    p = p_ref[...]
    row = lax.broadcasted_iota(jnp.int32, p.shape, 0)
    prev = jnp.where(row == 0, pp_ref[7:8, :], pltpu.roll(p, 1, 0))
    nxt = jnp.where(row == tm - 1, pn_ref[0:1, :], pltpu.roll(p, tm - 1, 0))
    xs = sc_ref[0:1, :] * prev + sc_ref[1:2, :] * p + sc_ref[2:3, :] * nxt
    c = RWKV_DIM
    r, k, v = xs[:, :c], xs[:, c:2 * c], xs[:, 2 * c:3 * c]
    lora = xs[:, 3 * c:3 * c + 4 * LORA]
    lane = lax.broadcasted_iota(jnp.int32, lora.shape, 1)
    lora = jnp.where(lane < 2 * LORA, jnp.tanh(lora), lora)
    lo = _dot(lora.astype(BF16), wl_ref[...])
    g = _dot(jax.nn.sigmoid(xs[:, 3 * c + 4 * LORA:]).astype(BF16), gup_ref[...])
    ones = ones_ref[...]
    kk = k * kkw_ref[...]
    nrm = jnp.sqrt(_dot2(kk * kk, ones))
    kk = kk / jnp.maximum(nrm, 1e-12)
    grow = pl.program_id(1) * tm + lax.broadcasted_iota(jnp.int32, v.shape, 0)
    v = jnp.where((grow < seq) | (grow >= seq + PAD_ROWS), v, 0.0)
    ka = ka_ref[...]
    kts = []
    for d, (lw_ref, kt_ref, b_ref) in enumerate(((lwf_ref, ktf_ref, bf_ref), (lwb_ref, ktb_ref, bb_ref))):
        w_log = -jax.nn.softplus(-(w0_ref[d:d + 1, :] + lo[:, d * c:(d + 1) * c])) - 0.5
        lw_ref[...] = -jnp.exp(w_log)
        iclr = jax.nn.sigmoid(a0_ref[d:d + 1, :] + lo[:, (2 + d) * c:(3 + d) * c])
        kt = k * (1.0 + (iclr - 1.0) * ka)
        kt_ref[...] = kt
        b_ref[...] = kk * iclr
        kts.append(kt)
    r_ref[...] = r
    v_ref[...] = v
    kk_ref[...] = kk
    g_ref[...] = g
    bonus_ref[...] = _dot2(r * (0.5 * (kts[0] + kts[1])) * rk_ref[...], ones) * v


def _rwkv_prep(pr, shift_c, w_lora, w0, a0, g_up, k_k, k_a, r_k, ones_blk, seq, tm):
    b, l, cols = pr.shape
    c = RWKV_DIM
    nb8 = l // 8
    row = lambda bi, i: (bi, i, 0)
    full = lambda a: pl.BlockSpec(a.shape, lambda bi, i: (0,) * a.ndim)
    out = jax.ShapeDtypeStruct((b, l, c), F32)
    ospec = pl.BlockSpec((None, tm, c), row)
    return pl.pallas_call(
        functools.partial(_rwkv_prep_kernel, seq=seq, tm=tm), grid=(b, l // tm),
        in_specs=[pl.BlockSpec((None, tm, cols), row),
                  pl.BlockSpec((None, 8, cols), lambda bi, i: (bi, (i * (tm // 8) + nb8 - 1) % nb8, 0)),
                  pl.BlockSpec((None, 8, cols), lambda bi, i: (bi, ((i + 1) * (tm // 8)) % nb8, 0)),
                  full(shift_c), full(w_lora), full(w0), full(a0), full(g_up), full(k_k), full(k_a), full(r_k),
                  full(ones_blk)],
        out_specs=[ospec] * 11, out_shape=[out] * 11,
        compiler_params=_cp(("parallel", "parallel")), name="rwkv_prep",
    )(pr, pr, pr, shift_c, w_lora, w0, a0, g_up, k_k, k_a, r_k, ones_blk)


def _scan_kernel(r_ref, lw_ref, kt_ref, v_ref, kk_ref, b_ref, tri_ref, y_ref, ht_ref, *, reverse):
    @pl.when(pl.program_id(1) == 0)
    def _():
        ht_ref[...] = jnp.zeros_like(ht_ref)

    lw = lw_ref[...]
    hi, lo = _split_bf16(lw)
    tri = tri_ref[...]
    cum = _dot(tri, hi) + _dot(tri, lo)
    tot = cum[0:1] if reverse else cum[CHUNK - 1:CHUNK]
    e_in = jnp.exp(cum)
    e_inv = jnp.exp(-cum)
    e_ex = jnp.exp(cum - lw)
    e_tot = jnp.exp(tot)
    at_all = -(kk_ref[...] * e_ex)
    rt_all = r_ref[...] * e_in
    bt_all = b_ref[...] * e_inv
    kt_all = kt_ref[...] * e_inv
    v_all = v_ref[...]

    n2 = 2 * CHUNK
    ri = lax.broadcasted_iota(jnp.int32, (n2, n2), 0)
    ci = lax.broadcasted_iota(jnp.int32, (n2, n2), 1)
    tt, ss = ri % CHUNK, ci % CHUNK
    strict = (tt < ss) if reverse else (tt > ss)
    incl = (tt <= ss) if reverse else (tt >= ss)
    bd = (ri < CHUNK) == (ci < CHUNK)
    eye = (ri == ci).astype(F32)
    head0 = lax.broadcasted_iota(jnp.int32, (CHUNK, n2), 1) < RWKV_HEAD
    cat = jnp.concatenate
    bf = lambda t: t.astype(BF16)

    for p in range(RWKV_HEADS // 2):
        sl = slice(p * n2, (p + 1) * n2)
        at, rt, bt, kt, v = at_all[:, sl], rt_all[:, sl], bt_all[:, sl], kt_all[:, sl], v_all[:, sl]
        at0, rt0, v0 = jnp.where(head0, at, 0.0), jnp.where(head0, rt, 0.0), jnp.where(head0, v, 0.0)
        at1, rt1, v1 = at - at0, rt - rt0, v - v0
        g0 = _dot_nt(bf(cat([at0, rt0], 0)), bf(cat([bt, kt], 0)))
        g1 = _dot_nt(bf(cat([at1, rt1], 0)), bf(cat([kt, bt], 0)))
        ga = jnp.where(strict, cat([g0[:CHUNK], g1[:CHUNK]], 0), 0.0)
        gr = jnp.where(incl, cat([g0[CHUNK:], g1[CHUNK:]], 0), 0.0)
        pk = jnp.where(bd, ga, 0.0)
        ga_anti = ga - pk
        tinv = eye + pk
        for _ in range(int(math.log2(CHUNK)) - 1):
            pk = _dot(bf(pk), bf(pk))
            tinv = tinv + _dot(bf(tinv), bf(pk))
        ht = bf(ht_ref[p])
        vx = bf(cat([v1, v0], 0))
        ws = _dot_nt(bf(cat([at0, at1], 0)), ht) + _dot(bf(ga_anti), vx)
        us = _dot(bf(tinv), bf(ws))
        gr_d = jnp.where(bd, gr, 0.0)
        ys = _dot_nt(bf(cat([rt0, rt1], 0)), ht) + _dot(bf(cat([gr_d, gr - gr_d], 1)), cat([bf(us), vx], 0))
        y_ref[:, sl] = ys[:CHUNK] + ys[CHUNK:]
        u = us[:CHUNK] + us[CHUNK:]
        et = e_tot[:, sl]
        hn = _dot(bf(cat([u, v], 0).T), bf(cat([bt * et, kt * et], 0)))
        ht_ref[p] = et * ht_ref[p] + jnp.where(bd, hn, 0.0)


def _scan(r, lw, kt, v, kk, bb, tri, seq, reverse):
    b, l, c = r.shape
    nmem = l // CHUNK
    nx = seq // CHUNK
    nc = nmem
    if reverse:
        cmap = lambda bi, ci: (bi, jnp.where(ci < nx, nx - 1 - ci, nx + nmem - 1 - ci), 0)
    else:
        cmap = lambda bi, ci: (bi, (ci + nmem - 1) % nmem, 0)
    spec = pl.BlockSpec((None, CHUNK, c), cmap)
    return pl.pallas_call(
        functools.partial(_scan_kernel, reverse=reverse), grid=(b, nc),
        in_specs=[spec] * 6 + [pl.BlockSpec((CHUNK, CHUNK), lambda bi, ci: (0, 0))],
        out_specs=spec, out_shape=jax.ShapeDtypeStruct((b, l, c), F32),
        scratch_shapes=[pltpu.VMEM((RWKV_HEADS // 2, 2 * CHUNK, 2 * CHUNK), F32)],
        compiler_params=_cp(("parallel", "arbitrary")), name="wkv_bwd" if reverse else "wkv_fwd",
    )(r, lw, kt, v, kk, bb, tri)


def _q_proj_kernel(x_ref, g_ref, w_ref, cs_ref, sn_ref, o_ref, nb_ref, *, scale):
    @pl.when(pl.program_id(2) == 0)
    def _():
        nb_ref[...] = _norm_rows(x_ref[...], g_ref[...]).astype(BF16)

    acc = _dot(nb_ref[...], w_ref[...])
    o_ref[:, :QK_NOPE] = (acc[:, :QK_NOPE] * scale).astype(o_ref.dtype)
    rot = acc[:, QK_NOPE:2 * QK_NOPE] * cs_ref[...] + acc[:, 2 * QK_NOPE:] * sn_ref[...]
    o_ref[:, QK_NOPE:] = (rot * scale).astype(o_ref.dtype)


def _q_proj(pm, g, wq, cs, sn, seq, tm, scale):
    b = pm.shape[0]
    k = g.shape[1]
    return pl.pallas_call(
        functools.partial(_q_proj_kernel, scale=scale), grid=(b, seq // tm, MLA_HEADS),
        in_specs=[pl.BlockSpec((None, tm, k), lambda bi, i, h: (bi, i, 0)),
                  pl.BlockSpec((1, k), lambda bi, i, h: (0, 0)),
                  pl.BlockSpec((None, k, 3 * QK_NOPE), lambda bi, i, h: (h, 0, 0)),
                  pl.BlockSpec((tm, QK_NOPE), lambda bi, i, h: (i, 0)),
                  pl.BlockSpec((tm, QK_NOPE), lambda bi, i, h: (i, 0))],
        out_specs=pl.BlockSpec((None, None, tm, QK_PAD), lambda bi, i, h: (bi, h, i, 0)),
        out_shape=jax.ShapeDtypeStruct((b, MLA_HEADS, seq, QK_PAD), BF16),
        scratch_shapes=[pltpu.VMEM((tm, k), BF16)],
        compiler_params=_cp(("parallel", "parallel", "arbitrary")), name="mla_q")(pm, g, wq, cs, sn)


def _kv_proj_kernel(x_ref, g_ref, w_ref, ka_ref, kb_ref, cs_ref, sn_ref, k_ref, v_ref, nb_ref):
    @pl.when(pl.program_id(2) == 0)
    def _():
        nb_ref[...] = _norm_rows(x_ref[...], g_ref[...]).astype(BF16)

    acc = _dot(nb_ref[...], w_ref[...])
    k_ref[:, :QK_NOPE] = acc[:, :QK_NOPE].astype(k_ref.dtype)
    k_ref[:, QK_NOPE:] = (ka_ref[...] * cs_ref[...] + kb_ref[...] * sn_ref[...]).astype(k_ref.dtype)
    v_ref[...] = acc[:, QK_NOPE:].astype(v_ref.dtype)


def _kv_proj(pm, g, wkv, cs, sn, tm):
    b, l, _ = pm.shape
    k = g.shape[1]
    kb = k // QK_NOPE
    return pl.pallas_call(
        _kv_proj_kernel, grid=(b, l // tm, MLA_HEADS),
        in_specs=[pl.BlockSpec((None, tm, k), lambda bi, i, h: (bi, i, 1)),
                  pl.BlockSpec((1, k), lambda bi, i, h: (0, 0)),
                  pl.BlockSpec((None, k, QK_NOPE + V_HEAD), lambda bi, i, h: (h, 0, 0)),
                  pl.BlockSpec((None, tm, QK_NOPE), lambda bi, i, h: (bi, i, 2 * kb)),
                  pl.BlockSpec((None, tm, QK_NOPE), lambda bi, i, h: (bi, i, 2 * kb + 1)),
                  pl.BlockSpec((tm, QK_NOPE), lambda bi, i, h: (i, 0)),
                  pl.BlockSpec((tm, QK_NOPE), lambda bi, i, h: (i, 0))],
        out_specs=[pl.BlockSpec((None, None, tm, QK_PAD), lambda bi, i, h: (bi, h, i, 0)),
                   pl.BlockSpec((None, None, tm, V_HEAD), lambda bi, i, h: (bi, h, i, 0))],
        out_shape=[jax.ShapeDtypeStruct((b, MLA_HEADS, l, QK_PAD), BF16),
                   jax.ShapeDtypeStruct((b, MLA_HEADS, l, V_HEAD), BF16)],
        scratch_shapes=[pltpu.VMEM((tm, k), BF16)],
        compiler_params=_cp(("parallel", "parallel", "arbitrary")), name="mla_kv")(pm, g, wkv, pm, pm, cs, sn)


def _attn_kernel(q_ref, k_ref, v_ref, bias_ref, o_ref, *, seq):
    q = q_ref[...]
    s_x = _dot_nt(q, k_ref[:seq, :])
    s_t = _dot_nt(q, k_ref[seq:, :]) + bias_ref[...]
    m = jnp.maximum(jnp.max(s_x, axis=-1, keepdims=True), jnp.max(s_t, axis=-1, keepdims=True))
    p_x = jnp.exp(s_x - m)
    p_t = jnp.exp(s_t - m)
    den = jnp.sum(p_x, axis=-1, keepdims=True) + jnp.sum(p_t, axis=-1, keepdims=True)
    o = _dot(p_x.astype(BF16), v_ref[:seq, :]) + _dot(p_t.astype(BF16), v_ref[seq:, :])
    o_ref[...] = (o / den).astype(o_ref.dtype)


def _attention(q, k, v, bias, seq, bq):
    b, h, l, _ = k.shape
    return pl.pallas_call(
        functools.partial(_attn_kernel, seq=seq), grid=(b, h, seq // bq),
        in_specs=[pl.BlockSpec((None, None, bq, QK_PAD), lambda bi, hi, i: (bi, hi, i, 0)),
                  pl.BlockSpec((None, None, l, QK_PAD), lambda bi, hi, i: (bi, hi, 0, 0)),
                  pl.BlockSpec((None, None, l, V_HEAD), lambda bi, hi, i: (bi, hi, 0, 0)),
                  pl.BlockSpec((1, TAIL), lambda bi, hi, i: (0, 0))],
        out_specs=pl.BlockSpec((None, bq, V_HEAD), lambda bi, hi, i: (bi, i, hi)),
        out_shape=jax.ShapeDtypeStruct((b, seq, h * V_HEAD), BF16),
        compiler_params=_cp(("parallel", "parallel", "arbitrary")), name="mla_attn")(q, k, v, bias)


def _merge_kernel(yf_ref, yb_ref, bonus_ref, g_ref, lng_ref, lnb_ref, avg_ref, ym_ref, gr_ref, gm_ref,
                  bgr_ref, bgm_ref, pr_ref, pm_ref, o_ref, yr_ref):
    @pl.when(pl.program_id(2) == 0)
    def _():
        y = yf_ref[...] + yb_ref[...]
        avg = avg_ref[...]
        d = y - _dot2(y, avg)
        var = _dot2(d * d, avg)
        yn = d * lax.rsqrt(var + GN_EPS) * lng_ref[...] + lnb_ref[...]
        yr_ref[...] = ((yn + bonus_ref[...]) * g_ref[...]).astype(BF16)

    a_r = _dot(yr_ref[...], pr_ref[...])
    a_m = _dot(ym_ref[...], pm_ref[...])
    merged = jax.nn.sigmoid(gr_ref[...] + bgr_ref[...]) * a_r + jax.nn.sigmoid(gm_ref[...] + bgm_ref[...]) * a_m
    o_ref[...] = merged.astype(o_ref.dtype)


def _merge(yf, yb, bonus, g, ln_g, ln_b, avg_blk, ym, pg, b_gate, p_rwkv, p_mla, seq, tm, tn):
    b = yf.shape[0]
    c = RWKV_DIM
    d = p_rwkv.shape[1]
    nj = d // tn
    rowc = pl.BlockSpec((None, tm, c), lambda bi, i, j: (bi, i, 0))
    vec = pl.BlockSpec((1, c), lambda bi, i, j: (0, 0))
    return pl.pallas_call(
        _merge_kernel, grid=(b, seq // tm, nj),
        in_specs=[rowc, rowc, rowc, rowc, vec, vec,
                  pl.BlockSpec((c, c), lambda bi, i, j: (0, 0)),
                  pl.BlockSpec((None, tm, ym.shape[2]), lambda bi, i, j: (bi, i, 0)),
                  pl.BlockSpec((None, tm, tn), lambda bi, i, j: (bi, i, j)),
                  pl.BlockSpec((None, tm, tn), lambda bi, i, j: (bi, i, j + nj)),
                  pl.BlockSpec((1, tn), lambda bi, i, j: (0, j)),
                  pl.BlockSpec((1, tn), lambda bi, i, j: (0, j + nj)),
                  pl.BlockSpec((c, tn), lambda bi, i, j: (0, j)),
                  pl.BlockSpec((p_mla.shape[0], tn), lambda bi, i, j: (0, j))],
        out_specs=pl.BlockSpec((None, tm, tn), lambda bi, i, j: (bi, i, j)),
        out_shape=jax.ShapeDtypeStruct((b, seq, d), BF16),
        scratch_shapes=[pltpu.VMEM((tm, c), BF16)],
        compiler_params=_cp(("parallel", "parallel", "arbitrary")), name="merge",
    )(yf, yb, bonus, g, ln_g, ln_b, avg_blk, ym, pg, pg, b_gate, b_gate, p_rwkv, p_mla)


def _out_proj_kernel(a_ref, w_ref, h_ref, o_ref):
    o_ref[...] = h_ref[...] + _dot(a_ref[...], w_ref[...])


def _out_proj(merged, w_o, hp, seq, tm, tn):
    b, _, d = merged.shape
    return pl.pallas_call(
        _out_proj_kernel, grid=(b, seq // tm, d // tn),
        in_specs=[pl.BlockSpec((None, tm, d), lambda bi, i, j: (bi, i, 0)),
                  pl.BlockSpec((d, tn), lambda bi, i, j: (0, j)),
                  pl.BlockSpec((None, tm, tn), lambda bi, i, j: (bi, i, j))],
        out_specs=pl.BlockSpec((None, tm, tn), lambda bi, i, j: (bi, i, j)),
        out_shape=jax.ShapeDtypeStruct((b, seq, d), F32),
        compiler_params=_cp(("parallel", "parallel", "arbitrary")), name="out_proj")(merged, w_o, hp)


def _topk_rows(s, ids, k):
    n = s.shape[0]
    rows = lax.broadcasted_iota(jnp.int32, s.shape, 0)
    vals, sel = [], []
    for _ in range(k):
        m = jnp.max(s, axis=0, keepdims=True)
        first = jnp.min(jnp.where(s == m, rows, n), axis=0, keepdims=True)
        hit = rows == first
        vals.append(m)
        sel.append(jnp.max(jnp.where(hit, ids, -1), axis=0, keepdims=True))
        s = jnp.where(hit, -jnp.inf, s)
    return jnp.concatenate(vals, 0), jnp.concatenate(sel, 0)


def _route_kernel(h_ref, g_ref, wq_ref, keys_ref, e_ref, gate_ref):
    nb = _norm_rows(h_ref[...], g_ref[...]).astype(BF16)
    q = _dot(nb, wq_ref[...]).astype(BF16)
    tm = q.shape[0]
    key_rows = lax.broadcasted_iota(jnp.int32, (N_KEYS, tm), 0)
    e_rows, g_rows = [], []
    for h in range(PEER_HEADS):
        tops = []
        for half in range(2):
            gidx = 2 * h + half
            s = _dot_nt(keys_ref[gidx], q[:, gidx * N_KEYS:(gidx + 1) * N_KEYS])
            tops.append(_topk_rows(s, key_rows, PEER_TOPK))
        (s1, i1), (s2, i2) = tops
        cand = jnp.concatenate([s1[a:a + 1] + s2 for a in range(PEER_TOPK)], 0)
        eid = jnp.concatenate([i1[a:a + 1] * N_KEYS + i2 for a in range(PEER_TOPK)], 0)
        best, experts = _topk_rows(cand, eid, PEER_TOPK)
        ex = jnp.exp(best - best[0:1])
        g_rows.append(ex / jnp.sum(ex, axis=0, keepdims=True))
        e_rows.append(experts)
    e_ref[...] = jnp.concatenate(e_rows, 0).T
    gate_ref[...] = jnp.concatenate(g_rows, 0).T


def _route(h2, g, wq, keys, tm):
    t, d = h2.shape
    return pl.pallas_call(
        _route_kernel, grid=(t // tm,),
        in_specs=[pl.BlockSpec((tm, d), lambda i: (i, 0)),
                  pl.BlockSpec((1, d), lambda i: (0, 0)),
                  pl.BlockSpec(wq.shape, lambda i: (0, 0)),
                  pl.BlockSpec(keys.shape, lambda i: (0, 0, 0))],
        out_specs=[pl.BlockSpec((tm, PEER_SEL), lambda i: (i, 0))] * 2,
        out_shape=[jax.ShapeDtypeStruct((t, PEER_SEL), jnp.int32), jax.ShapeDtypeStruct((t, PEER_SEL), F32)],
        compiler_params=_cp(("parallel",)), name="peer_route")(h2, g, wq, keys)


def _expert_kernel(idx_ref, idxn_ref, h_ref, gate_ref, gffn_ref, gfin_ref, tab_ref, o_ref, buf_ref, sem_ref):
    i = pl.program_id(0)
    n = pl.num_programs(0)
    slot = i % 2
    rows = PEER_TOK * PEER_SEL

    def issue(src_ref, dst_slot):
        def body(r, carry):
            e = src_ref[0, 0, r]
            pltpu.make_async_copy(tab_ref.at[pl.ds(e, 1)], buf_ref.at[dst_slot, pl.ds(r, 1)],
                                  sem_ref.at[dst_slot]).start()
            return carry
        lax.fori_loop(0, rows, body, 0, unroll=8)

    @pl.when(i == 0)
    def _():
        issue(idx_ref, 0)

    @pl.when(i + 1 < n)
    def _():
        issue(idxn_ref, 1 - slot)

    pltpu.make_async_copy(tab_ref.at[pl.ds(0, rows)], buf_ref.at[slot], sem_ref.at[slot]).wait()

    w = buf_ref[slot]
    u = pltpu.bitcast(w << 16, F32).astype(BF16)
    v = pltpu.bitcast(w & jnp.uint32(0xFFFF0000), F32).astype(BF16)
    h = h_ref[...]
    nb = _norm_rows(h, gffn_ref[...]).astype(BF16)
    s = _dot_nt(nb, u)
    gate = gate_ref[...]
    trow = lax.broadcasted_iota(jnp.int32, gate.shape, 0)
    gmat = jnp.concatenate([jnp.where(trow == t, gate, 0.0) for t in range(PEER_TOK)], axis=1)
    act = 0.5 * s * (1.0 + lax.erf(s * (2.0 ** -0.5))) * gmat
    y = h + _dot(act.astype(BF16), v)
    o_ref[...] = _norm_rows(y, gfin_ref[...])


def _experts(idx, h2, gate, g_ffn, g_fin, table):
    t, d = h2.shape
    nt = t // PEER_TOK
    rows = PEER_TOK * PEER_SEL
    idx3 = idx.reshape(nt, 1, rows)
    return pl.pallas_call(
        _expert_kernel, grid=(nt,),
        in_specs=[pl.BlockSpec((1, 1, rows), lambda i: (i, 0, 0), memory_space=pltpu.SMEM),
                  pl.BlockSpec((1, 1, rows), lambda i: (jnp.minimum(i + 1, nt - 1), 0, 0), memory_space=pltpu.SMEM),
                  pl.BlockSpec((PEER_TOK, d), lambda i: (i, 0)),
                  pl.BlockSpec((PEER_TOK, PEER_SEL), lambda i: (i, 0)),
                  pl.BlockSpec((1, d), lambda i: (0, 0)),
                  pl.BlockSpec((1, d), lambda i: (0, 0)),
                  pl.BlockSpec(memory_space=pl.ANY)],
        out_specs=pl.BlockSpec((PEER_TOK, d), lambda i: (i, 0)),
        out_shape=jax.ShapeDtypeStruct((t, d), F32),
        scratch_shapes=[pltpu.VMEM((2, rows, d), jnp.uint32), pltpu.SemaphoreType.DMA((2,))],
        compiler_params=_cp(("arbitrary",)), name="peer_experts")(idx3, idx3, h2, gate, g_ffn, g_fin, table)


def _block_const(n, blk, val):
    i = jnp.arange(n) // blk
    return jnp.where(i[:, None] == i[None, :], val, 0.0).astype(BF16)


def _layer(hp, seq, prm, table):
    b, l, d = hp.shape
    c = RWKV_DIM
    bf = lambda a: a.astype(BF16)
    w_in = prm["w_in"]
    o_q = c * 3 + GATE_LORA + 4 * LORA
    q_lora = prm["q_norm_g"].shape[0]
    kv_lora = prm["kv_norm_g"].shape[0]
    o_kr = o_q + q_lora + kv_lora
    o_g = o_kr + QK_ROPE

    def rw_cols(a):
        lead = a.shape[:-1]
        return jnp.concatenate([a[..., :3 * c], a[..., 3 * c + GATE_LORA:o_q], a[..., 3 * c:3 * c + GATE_LORA],
                                jnp.zeros(lead + (2 * QK_NOPE - GATE_LORA,), a.dtype)], -1)

    half = QK_ROPE // 2
    swap = lambda a: jnp.concatenate([a[..., half:], a[..., :half]], -1)
    padk = lambda a: jnp.concatenate([a, jnp.zeros(a.shape[:-1] + (QK_NOPE - a.shape[-1],), a.dtype)], -1)
    w_kr = w_in[:, o_kr:o_g]
    w_rw = bf(rw_cols(w_in))
    w_mla = bf(jnp.concatenate([w_in[:, o_q:o_kr], padk(w_kr), padk(swap(w_kr))], -1))
    w_gate = bf(w_in[:, o_g:])
    g_mix = prm["norm_mix_g"][None]

    tm_all = _pick(l, (384, 128))
    pr = _norm_mm(hp, g_mix, w_rw, tm_all, _pick(w_rw.shape[1], (512, 256, 128)), name="in_rwkv")
    pm = _norm_mm(hp, g_mix, w_mla, tm_all, _pick(w_mla.shape[1], (640, 128)), name="in_mla")
    pg = _norm_mm(hp, g_mix, w_gate, tm_all, 512, name="in_gate")

    zl = jnp.zeros((LORA, c), F32)
    w_lora = bf(jnp.concatenate([
        jnp.concatenate([prm["w_up"][0], zl, zl, zl], 1), jnp.concatenate([zl, prm["w_up"][1], zl, zl], 1),
        jnp.concatenate([zl, zl, prm["a_up"][0], zl], 1), jnp.concatenate([zl, zl, zl, prm["a_up"][1]], 1)], 0))
    g_up = bf(jnp.concatenate([prm["g_up"], jnp.zeros((2 * QK_NOPE - GATE_LORA, c), F32)], 0))
    ones_blk = _block_const(c, RWKV_HEAD, 1.0)
    r, v, kk, lwf, lwb, ktf, ktb, bfw, bbw, g, bonus = _rwkv_prep(
        pr, rw_cols(prm["shift_c"]), w_lora, prm["w0"], prm["a0"], g_up, prm["k_k"][None], prm["k_a"][None],
        prm["r_k"].reshape(1, c), ones_blk, seq, 128)
    t_i = np.arange(CHUNK)
    tri_f = jnp.asarray(t_i[:, None] >= t_i[None, :], BF16)
    tri_b = jnp.asarray(t_i[:, None] <= t_i[None, :], BF16)
    y_f = _scan(r, lwf, ktf, v, kk, bfw, tri_f, seq, False)
    y_b = _scan(r, lwb, ktb, v, kk, bbw, tri_b, seq, True)

    mem = np.arange(l)
    pos = np.where(mem < seq, mem + N_META, np.maximum(mem - seq - PAD_ROWS, 0)).astype(np.float32)
    inv_freq = ROPE_THETA ** (-jnp.arange(0, QK_ROPE, 2, dtype=F32) / QK_ROPE)
    ang = jnp.asarray(pos)[:, None] * inv_freq[None, :]
    cos, sin = jnp.cos(ang), jnp.sin(ang)
    z2 = jnp.zeros((l, QK_NOPE - QK_ROPE), F32)
    cs = jnp.concatenate([cos, cos, z2], 1)
    sn = jnp.concatenate([-sin, sin, z2], 1)
    qh = prm["w_uq"].reshape(q_lora, MLA_HEADS, QK_NOPE + QK_ROPE).transpose(1, 0, 2)
    wq = bf(jnp.concatenate([qh[..., :QK_NOPE], padk(qh[..., QK_NOPE:]), padk(swap(qh[..., QK_NOPE:]))], -1))
    wkv = bf(prm["w_ukv"].reshape(kv_lora, MLA_HEADS, QK_NOPE + V_HEAD).transpose(1, 0, 2))
    tm_x = _pick(seq, (512, 256, 128))
    scale = (QK_NOPE + QK_ROPE) ** -0.5
    q = _q_proj(pm, prm["q_norm_g"][None], wq, cs, sn, seq, tm_x, scale)
    k, vv = _kv_proj(pm, prm["kv_norm_g"][None], wkv, cs, sn, tm_all)
    bias = jnp.asarray(np.where(np.arange(TAIL) < PAD_ROWS, -1e30, 0.0)[None], F32)
    y_m = _attention(q, k, vv, bias, seq, tm_x)

    avg_blk = _block_const(c, RWKV_HEAD, 1.0 / RWKV_HEAD)
    merged = _merge(y_f, y_b, bonus, g, prm["ln_x_g"][None], prm["ln_x_b"][None], avg_blk, y_m, pg,
                    prm["b_gate"][None], bf(prm["p_rwkv"]), bf(prm["p_mla"]), seq, tm_x, 512)
    h2 = _out_proj(merged, bf(prm["w_o"]), hp, seq, tm_x, 512).reshape(b * seq, d)

    keys = bf(prm["peer_keys"].reshape(PEER_HEADS * 2, N_KEYS, -1))
    g_ffn = prm["norm_ffn_g"][None]
    experts, gates = _route(h2, g_ffn, bf(prm["peer_wq"]), keys, _pick(b * seq, (256, 128)))
    return experts, gates, h2, g_ffn


def kernel(x, meta_tokens, norm_mix_g, w_in, b_gate, shift_c, w_up, w0, a_up, a0, g_up, k_k, k_a, r_k,
           ln_x_g, ln_x_b, q_norm_g, w_uq, kv_norm_g, w_ukv, p_rwkv, p_mla, w_o, norm_ffn_g,
           peer_wq, peer_keys, peer_u, peer_v, final_norm_g):
    b, seq, d = x.shape
    assert norm_mix_g.shape[0] == 1, "single-layer block"
    assert seq % 128 == 0 and meta_tokens.shape[0] == N_META
    meta = jnp.broadcast_to(meta_tokens.astype(x.dtype)[None], (b, N_META, d))
    hp = jnp.concatenate([x, jnp.zeros((b, PAD_ROWS, d), x.dtype), meta], axis=1)
    prm = dict(norm_mix_g=norm_mix_g[0], w_in=w_in[0], b_gate=b_gate[0], shift_c=shift_c[0], w_up=w_up[0],
               w0=w0[0], a_up=a_up[0], a0=a0[0], g_up=g_up[0], k_k=k_k[0], k_a=k_a[0], r_k=r_k[0],
               ln_x_g=ln_x_g[0], ln_x_b=ln_x_b[0], q_norm_g=q_norm_g[0], w_uq=w_uq[0], kv_norm_g=kv_norm_g[0],
               w_ukv=w_ukv[0], p_rwkv=p_rwkv[0], p_mla=p_mla[0], w_o=w_o[0], norm_ffn_g=norm_ffn_g[0],
               peer_wq=peer_wq[0], peer_keys=peer_keys[0])
    table = _pack_tables(peer_u[0], peer_v[0])
    experts, gates, h2, g_ffn = _layer(hp, seq, prm, table)
    out = _experts(experts, h2, gates, g_ffn, final_norm_g[None], table)
    return out.reshape(b, seq, d)
```

```python
import functools
import math

import jax
import jax.numpy as jnp
import numpy as np
from jax import lax
from jax.experimental import pallas as pl
from jax.experimental.pallas import tpu as pltpu

F32 = jnp.float32
BF16 = jnp.bfloat16

N_META = 16
PAD_ROWS = 112
TAIL = PAD_ROWS + N_META
NORM_EPS = 1e-6
GN_EPS = 64e-5
ROPE_THETA = 10000.0

RWKV_HEADS = 16
RWKV_HEAD = 64
RWKV_DIM = RWKV_HEADS * RWKV_HEAD
GATE_LORA = 160
LORA = 64
CHUNK = 64

MLA_HEADS = 16
QK_NOPE = 128
QK_ROPE = 64
V_HEAD = 128
QK_PAD = 256

N_KEYS = 128
PEER_HEADS = 8
PEER_TOPK = 16
PEER_SEL = PEER_HEADS * PEER_TOPK
PEER_TOK = 8

VMEM_LIMIT = 56 * 1024 * 1024


def _cp(sem, vmem=VMEM_LIMIT):
    return pltpu.CompilerParams(dimension_semantics=sem, vmem_limit_bytes=vmem)


def _pick(n, cands):
    for c in cands:
        if n % c == 0:
            return c
    raise ValueError(f"no tile for {n}")


def _dot(a, b):
    return jnp.dot(a, b, preferred_element_type=F32)


def _dot_nt(a, b):
    return lax.dot_general(a, b, (((1,), (1,)), ((), ())), preferred_element_type=F32)


def _split_bf16(x):
    hi = x.astype(BF16)
    lo = (x - hi.astype(F32)).astype(BF16)
    return hi, lo


def _dot2(x, w):
    hi, lo = _split_bf16(x)
    return _dot(hi, w) + _dot(lo, w)


def _pack_kernel(u_ref, v_ref, o_ref):
    ub = pltpu.bitcast(u_ref[...].astype(BF16).astype(F32), jnp.uint32)
    vb = pltpu.bitcast(v_ref[...].astype(BF16).astype(F32), jnp.uint32)
    o_ref[...] = (ub >> 16) | (vb & jnp.uint32(0xFFFF0000))


def _pack_tables(u, v):
    n, d = u.shape
    tr = _pick(n, (512, 256, 128, 8))
    spec = pl.BlockSpec((tr, d), lambda i: (i, 0))
    return pl.pallas_call(
        _pack_kernel, grid=(n // tr,), in_specs=[spec, spec], out_specs=spec,
        out_shape=jax.ShapeDtypeStruct((n, d), jnp.uint32),
        compiler_params=_cp(("parallel",)), name="peer_pack")(u, v)


def _norm_rows(x, g):
    ms = jnp.mean(x * x, axis=-1, keepdims=True)
    return x * lax.rsqrt(ms + NORM_EPS) * g


def _norm_mm_kernel(x_ref, g_ref, w_ref, o_ref, nb_ref):
    @pl.when(pl.program_id(2) == 0)
    def _():
        nb_ref[...] = _norm_rows(x_ref[...], g_ref[...]).astype(BF16)

    o_ref[...] = _dot(nb_ref[...], w_ref[...]).astype(o_ref.dtype)


def _norm_mm(x, g, w, tm, tn, out_dtype=F32, name="norm_mm"):
    b, l, k = x.shape
    n = w.shape[1]
    return pl.pallas_call(
        _norm_mm_kernel, grid=(b, l // tm, n // tn),
        in_specs=[pl.BlockSpec((None, tm, k), lambda bi, i, j: (bi, i, 0)),
                  pl.BlockSpec((1, k), lambda bi, i, j: (0, 0)),
                  pl.BlockSpec((k, tn), lambda bi, i, j: (0, j))],
        out_specs=pl.BlockSpec((None, tm, tn), lambda bi, i, j: (bi, i, j)),
        out_shape=jax.ShapeDtypeStruct((b, l, n), out_dtype),
        scratch_shapes=[pltpu.VMEM((tm, k), BF16)],
        compiler_params=_cp(("parallel", "parallel", "arbitrary")), name=name)(x, g, w)


def _rwkv_prep_kernel(p_ref, pp_ref, pn_ref, sc_ref, wl_ref, w0_ref, a0_ref, gup_ref, kkw_ref, ka_ref, rk_ref,
                      ones_ref, r_ref, v_ref, kk_ref, lwf_ref, lwb_ref, ktf_ref, ktb_ref, bf_ref, bb_ref,
                      g_ref, bonus_ref, *, seq, tm):
    p = p_ref[...]
    row = lax.broadcasted_iota(jnp.int32, p.shape, 0)
    prev = jnp.where(row == 0, pp_ref[7:8, :], pltpu.roll(p, 1, 0))
    nxt = jnp.where(row == tm - 1, pn_ref[0:1, :], pltpu.roll(p, tm - 1, 0))
    xs = sc_ref[0:1, :] * prev + sc_ref[1:2, :] * p + sc_ref[2:3, :] * nxt
    c = RWKV_DIM
    r, k, v = xs[:, :c], xs[:, c:2 * c], xs[:, 2 * c:3 * c]
    lora = xs[:, 3 * c:3 * c + 4 * LORA]
    lane = lax.broadcasted_iota(jnp.int32, lora.shape, 1)
    lora = jnp.where(lane < 2 * LORA, jnp.tanh(lora), lora)
    lo = _dot(lora.astype(BF16), wl_ref[...])
    g = _dot(jax.nn.sigmoid(xs[:, 3 * c + 4 * LORA:]).astype(BF16), gup_ref[...])
    ones = ones_ref[...]
    kk = k * kkw_ref[...]
    nrm = jnp.sqrt(_dot2(kk * kk, ones))
    kk = kk / jnp.maximum(nrm, 1e-12)
    grow = pl.program_id(1) * tm + lax.broadcasted_iota(jnp.int32, v.shape, 0)
    v = jnp.where((grow < seq) | (grow >= seq + PAD_ROWS), v, 0.0)
    ka = ka_ref[...]
    kts = []
    for d, (lw_ref, kt_ref, b_ref) in enumerate(((lwf_ref, ktf_ref, bf_ref), (lwb_ref, ktb_ref, bb_ref))):
        w_log = -jax.nn.softplus(-(w0_ref[d:d + 1, :] + lo[:, d * c:(d + 1) * c])) - 0.5
        lw_ref[...] = -jnp.exp(w_log)
        iclr = jax.nn.sigmoid(a0_ref[d:d + 1, :] + lo[:, (2 + d) * c:(3 + d) * c])
        kt = k * (1.0 + (iclr - 1.0) * ka)
        kt_ref[...] = kt
        b_ref[...] = kk * iclr
        kts.append(kt)
    r_ref[...] = r
    v_ref[...] = v
    kk_ref[...] = kk
    g_ref[...] = g
    bonus_ref[...] = _dot2(r * (0.5 * (kts[0] + kts[1])) * rk_ref[...], ones) * v


def _rwkv_prep(pr, shift_c, w_lora, w0, a0, g_up, k_k, k_a, r_k, ones_blk, seq, tm):
    b, l, cols = pr.shape
    c = RWKV_DIM
    nb8 = l // 8
    row = lambda bi, i: (bi, i, 0)
    full = lambda a: pl.BlockSpec(a.shape, lambda bi, i: (0,) * a.ndim)
    out = jax.ShapeDtypeStruct((b, l, c), F32)
    ospec = pl.BlockSpec((None, tm, c), row)
    return pl.pallas_call(
        functools.partial(_rwkv_prep_kernel, seq=seq, tm=tm), grid=(b, l // tm),
        in_specs=[pl.BlockSpec((None, tm, cols), row),
                  pl.BlockSpec((None, 8, cols), lambda bi, i: (bi, (i * (tm // 8) + nb8 - 1) % nb8, 0)),
                  pl.BlockSpec((None, 8, cols), lambda bi, i: (bi, ((i + 1) * (tm // 8)) % nb8, 0)),
                  full(shift_c), full(w_lora), full(w0), full(a0), full(g_up), full(k_k), full(k_a), full(r_k),
                  full(ones_blk)],
        out_specs=[ospec] * 11, out_shape=[out] * 11,
        compiler_params=_cp(("parallel", "parallel")), name="rwkv_prep",
    )(pr, pr, pr, shift_c, w_lora, w0, a0, g_up, k_k, k_a, r_k, ones_blk)


def _scan_kernel(r_ref, lw_ref, kt_ref, v_ref, kk_ref, b_ref, tri_ref, y_ref, ht_ref, *, reverse):
    @pl.when(pl.program_id(1) == 0)
    def _():
        ht_ref[...] = jnp.zeros_like(ht_ref)

    lw = lw_ref[...]
    hi, lo = _split_bf16(lw)
    tri = tri_ref[...]
    cum = _dot(tri, hi) + _dot(tri, lo)
    tot = cum[0:1] if reverse else cum[CHUNK - 1:CHUNK]
    e_in = jnp.exp(cum)
    e_inv = jnp.exp(-cum)
    e_ex = jnp.exp(cum - lw)
    e_tot = jnp.exp(tot)
    at_all = -(kk_ref[...] * e_ex)
    rt_all = r_ref[...] * e_in
    bt_all = b_ref[...] * e_inv
    kt_all = kt_ref[...] * e_inv
    v_all = v_ref[...]

    n2 = 2 * CHUNK
    ri = lax.broadcasted_iota(jnp.int32, (n2, n2), 0)
    ci = lax.broadcasted_iota(jnp.int32, (n2, n2), 1)
    tt, ss = ri % CHUNK, ci % CHUNK
    strict = (tt < ss) if reverse else (tt > ss)
    incl = (tt <= ss) if reverse else (tt >= ss)
    bd = (ri < CHUNK) == (ci < CHUNK)
    eye = (ri == ci).astype(F32)
    head0 = lax.broadcasted_iota(jnp.int32, (CHUNK, n2), 1) < RWKV_HEAD
    cat = jnp.concatenate
    bf = lambda t: t.astype(BF16)

    pairs = range(RWKV_HEADS // 2)
    sls = [slice(p * n2, (p + 1) * n2) for p in pairs]
    h_old = [ht_ref[p] for p in pairs]
    at = [at_all[:, sl] for sl in sls]
    rt = [rt_all[:, sl] for sl in sls]
    bt = [bt_all[:, sl] for sl in sls]
    kt = [kt_all[:, sl] for sl in sls]
    v = [v_all[:, sl] for sl in sls]
    at0 = [jnp.where(head0, t, 0.0) for t in at]
    rt0 = [jnp.where(head0, t, 0.0) for t in rt]
    v0 = [jnp.where(head0, t, 0.0) for t in v]
    g0 = [_dot_nt(bf(cat([at0[p], rt0[p]], 0)), bf(cat([bt[p], kt[p]], 0))) for p in pairs]
    g1 = [_dot_nt(bf(cat([at[p] - at0[p], rt[p] - rt0[p]], 0)), bf(cat([kt[p], bt[p]], 0))) for p in pairs]
    ga = [jnp.where(strict, cat([g0[p][:CHUNK], g1[p][:CHUNK]], 0), 0.0) for p in pairs]
    gr = [jnp.where(incl, cat([g0[p][CHUNK:], g1[p][CHUNK:]], 0), 0.0) for p in pairs]
    pk = [jnp.where(bd, t, 0.0) for t in ga]
    ga_anti = [bf(ga[p] - pk[p]) for p in pairs]
    tinv = [eye + t for t in pk]
    for _ in range(int(math.log2(CHUNK)) - 1):
        pk = [_dot(bf(t), bf(t)) for t in pk]
        tinv = [tinv[p] + _dot(bf(tinv[p]), bf(pk[p])) for p in pairs]
    ht = [bf(t) for t in h_old]
    vx = [bf(cat([v[p] - v0[p], v0[p]], 0)) for p in pairs]
    ws = [_dot_nt(bf(cat([at0[p], at[p] - at0[p]], 0)), ht[p]) + _dot(ga_anti[p], vx[p]) for p in pairs]
    us = [_dot(bf(tinv[p]), bf(ws[p])) for p in pairs]
    gr_d = [jnp.where(bd, t, 0.0) for t in gr]
    ys = [_dot_nt(bf(cat([rt0[p], rt[p] - rt0[p]], 0)), ht[p])
          + _dot(bf(cat([gr_d[p], gr[p] - gr_d[p]], 1)), cat([bf(us[p]), vx[p]], 0)) for p in pairs]
    et = [e_tot[:, sl] for sl in sls]
    hn = [_dot(bf(cat([us[p][:CHUNK] + us[p][CHUNK:], v[p]], 0).T), bf(cat([bt[p] * et[p], kt[p] * et[p]], 0)))
          for p in pairs]
    for p in pairs:
        y_ref[:, sls[p]] = ys[p][:CHUNK] + ys[p][CHUNK:]
        ht_ref[p] = et[p] * h_old[p] + jnp.where(bd, hn[p], 0.0)


def _scan(r, lw, kt, v, kk, bb, tri, seq, reverse):
    b, l, c = r.shape
    nmem = l // CHUNK
    nx = seq // CHUNK
    nc = nmem
    if reverse:
        cmap = lambda bi, ci: (bi, jnp.where(ci < nx, nx - 1 - ci, nx + nmem - 1 - ci), 0)
    else:
        cmap = lambda bi, ci: (bi, (ci + nmem - 1) % nmem, 0)
    spec = pl.BlockSpec((None, CHUNK, c), cmap)
    return pl.pallas_call(
        functools.partial(_scan_kernel, reverse=reverse), grid=(b, nc),
        in_specs=[spec] * 6 + [pl.BlockSpec((CHUNK, CHUNK), lambda bi, ci: (0, 0))],
        out_specs=spec, out_shape=jax.ShapeDtypeStruct((b, l, c), F32),
        scratch_shapes=[pltpu.VMEM((RWKV_HEADS // 2, 2 * CHUNK, 2 * CHUNK), F32)],
        compiler_params=_cp(("parallel", "arbitrary")), name="wkv_bwd" if reverse else "wkv_fwd",
    )(r, lw, kt, v, kk, bb, tri)


def _q_proj_kernel(x_ref, g_ref, w_ref, cs_ref, sn_ref, o_ref, nb_ref, *, scale):
    @pl.when(pl.program_id(2) == 0)
    def _():
        nb_ref[...] = _norm_rows(x_ref[...], g_ref[...]).astype(BF16)

    acc = _dot(nb_ref[...], w_ref[...])
    o_ref[:, :QK_NOPE] = (acc[:, :QK_NOPE] * scale).astype(o_ref.dtype)
    rot = acc[:, QK_NOPE:2 * QK_NOPE] * cs_ref[...] + acc[:, 2 * QK_NOPE:] * sn_ref[...]
    o_ref[:, QK_NOPE:] = (rot * scale).astype(o_ref.dtype)


def _q_proj(pm, g, wq, cs, sn, seq, tm, scale):
    b = pm.shape[0]
    k = g.shape[1]
    return pl.pallas_call(
        functools.partial(_q_proj_kernel, scale=scale), grid=(b, seq // tm, MLA_HEADS),
        in_specs=[pl.BlockSpec((None, tm, k), lambda bi, i, h: (bi, i, 0)),
                  pl.BlockSpec((1, k), lambda bi, i, h: (0, 0)),
                  pl.BlockSpec((None, k, 3 * QK_NOPE), lambda bi, i, h: (h, 0, 0)),
                  pl.BlockSpec((tm, QK_NOPE), lambda bi, i, h: (i, 0)),
                  pl.BlockSpec((tm, QK_NOPE), lambda bi, i, h: (i, 0))],
        out_specs=pl.BlockSpec((None, None, tm, QK_PAD), lambda bi, i, h: (bi, h, i, 0)),
        out_shape=jax.ShapeDtypeStruct((b, MLA_HEADS, seq, QK_PAD), BF16),
        scratch_shapes=[pltpu.VMEM((tm, k), BF16)],
        compiler_params=_cp(("parallel", "parallel", "arbitrary")), name="mla_q")(pm, g, wq, cs, sn)


def _kv_proj_kernel(x_ref, g_ref, w_ref, ka_ref, kb_ref, cs_ref, sn_ref, k_ref, v_ref, nb_ref):
    @pl.when(pl.program_id(2) == 0)
    def _():
        nb_ref[...] = _norm_rows(x_ref[...], g_ref[...]).astype(BF16)

    acc = _dot(nb_ref[...], w_ref[...])
    k_ref[:, :QK_NOPE] = acc[:, :QK_NOPE].astype(k_ref.dtype)
    k_ref[:, QK_NOPE:] = (ka_ref[...] * cs_ref[...] + kb_ref[...] * sn_ref[...]).astype(k_ref.dtype)
    v_ref[...] = acc[:, QK_NOPE:].astype(v_ref.dtype)


def _kv_proj(pm, g, wkv, cs, sn, tm):
    b, l, _ = pm.shape
    k = g.shape[1]
    kb = k // QK_NOPE
    return pl.pallas_call(
        _kv_proj_kernel, grid=(b, l // tm, MLA_HEADS),
        in_specs=[pl.BlockSpec((None, tm, k), lambda bi, i, h: (bi, i, 1)),
                  pl.BlockSpec((1, k), lambda bi, i, h: (0, 0)),
                  pl.BlockSpec((None, k, QK_NOPE + V_HEAD), lambda bi, i, h: (h, 0, 0)),
                  pl.BlockSpec((None, tm, QK_NOPE), lambda bi, i, h: (bi, i, 2 * kb)),
                  pl.BlockSpec((None, tm, QK_NOPE), lambda bi, i, h: (bi, i, 2 * kb + 1)),
                  pl.BlockSpec((tm, QK_NOPE), lambda bi, i, h: (i, 0)),
                  pl.BlockSpec((tm, QK_NOPE), lambda bi, i, h: (i, 0))],
        out_specs=[pl.BlockSpec((None, None, tm, QK_PAD), lambda bi, i, h: (bi, h, i, 0)),
                   pl.BlockSpec((None, None, tm, V_HEAD), lambda bi, i, h: (bi, h, i, 0))],
        out_shape=[jax.ShapeDtypeStruct((b, MLA_HEADS, l, QK_PAD), BF16),
                   jax.ShapeDtypeStruct((b, MLA_HEADS, l, V_HEAD), BF16)],
        scratch_shapes=[pltpu.VMEM((tm, k), BF16)],
        compiler_params=_cp(("parallel", "parallel", "arbitrary")), name="mla_kv")(pm, g, wkv, pm, pm, cs, sn)


def _attn_kernel(q_ref, k_ref, v_ref, bias_ref, o_ref, *, seq):
    q = q_ref[...]
    s_x = _dot_nt(q, k_ref[:seq, :])
    s_t = _dot_nt(q, k_ref[seq:, :]) + bias_ref[...]
    m = jnp.maximum(jnp.max(s_x, axis=-1, keepdims=True), jnp.max(s_t, axis=-1, keepdims=True))
    p_x = jnp.exp(s_x - m)
    p_t = jnp.exp(s_t - m)
    den = jnp.sum(p_x, axis=-1, keepdims=True) + jnp.sum(p_t, axis=-1, keepdims=True)
    o = _dot(p_x.astype(BF16), v_ref[:seq, :]) + _dot(p_t.astype(BF16), v_ref[seq:, :])
    o_ref[...] = (o / den).astype(o_ref.dtype)


def _attention(q, k, v, bias, seq, bq):
    b, h, l, _ = k.shape
    return pl.pallas_call(
        functools.partial(_attn_kernel, seq=seq), grid=(b, h, seq // bq),
        in_specs=[pl.BlockSpec((None, None, bq, QK_PAD), lambda bi, hi, i: (bi, hi, i, 0)),
                  pl.BlockSpec((None, None, l, QK_PAD), lambda bi, hi, i: (bi, hi, 0, 0)),
                  pl.BlockSpec((None, None, l, V_HEAD), lambda bi, hi, i: (bi, hi, 0, 0)),
                  pl.BlockSpec((1, TAIL), lambda bi, hi, i: (0, 0))],
        out_specs=pl.BlockSpec((None, bq, V_HEAD), lambda bi, hi, i: (bi, i, hi)),
        out_shape=jax.ShapeDtypeStruct((b, seq, h * V_HEAD), BF16),
        compiler_params=_cp(("parallel", "parallel", "arbitrary")), name="mla_attn")(q, k, v, bias)


def _merge_kernel(yf_ref, yb_ref, bonus_ref, g_ref, lng_ref, lnb_ref, avg_ref, ym_ref, gr_ref, gm_ref,
                  bgr_ref, bgm_ref, pr_ref, pm_ref, o_ref, yr_ref):
    @pl.when(pl.program_id(2) == 0)
    def _():
        y = yf_ref[...] + yb_ref[...]
        avg = avg_ref[...]
        d = y - _dot2(y, avg)
        var = _dot2(d * d, avg)
        yn = d * lax.rsqrt(var + GN_EPS) * lng_ref[...] + lnb_ref[...]
        yr_ref[...] = ((yn + bonus_ref[...]) * g_ref[...]).astype(BF16)

    a_r = _dot(yr_ref[...], pr_ref[...])
    a_m = _dot(ym_ref[...], pm_ref[...])
    merged = jax.nn.sigmoid(gr_ref[...] + bgr_ref[...]) * a_r + jax.nn.sigmoid(gm_ref[...] + bgm_ref[...]) * a_m
    o_ref[...] = merged.astype(o_ref.dtype)


def _merge(yf, yb, bonus, g, ln_g, ln_b, avg_blk, ym, pg, b_gate, p_rwkv, p_mla, seq, tm, tn):
    b = yf.shape[0]
    c = RWKV_DIM
    d = p_rwkv.shape[1]
    nj = d // tn
    rowc = pl.BlockSpec((None, tm, c), lambda bi, i, j: (bi, i, 0))
    vec = pl.BlockSpec((1, c), lambda bi, i, j: (0, 0))
    return pl.pallas_call(
        _merge_kernel, grid=(b, seq // tm, nj),
        in_specs=[rowc, rowc, rowc, rowc, vec, vec,
                  pl.BlockSpec((c, c), lambda bi, i, j: (0, 0)),
                  pl.BlockSpec((None, tm, ym.shape[2]), lambda bi, i, j: (bi, i, 0)),
                  pl.BlockSpec((None, tm, tn), lambda bi, i, j: (bi, i, j)),
                  pl.BlockSpec((None, tm, tn), lambda bi, i, j: (bi, i, j + nj)),
                  pl.BlockSpec((1, tn), lambda bi, i, j: (0, j)),
                  pl.BlockSpec((1, tn), lambda bi, i, j: (0, j + nj)),
                  pl.BlockSpec((c, tn), lambda bi, i, j: (0, j)),
                  pl.BlockSpec((p_mla.shape[0], tn), lambda bi, i, j: (0, j))],
        out_specs=pl.BlockSpec((None, tm, tn), lambda bi, i, j: (bi, i, j)),
        out_shape=jax.ShapeDtypeStruct((b, seq, d), BF16),
        scratch_shapes=[pltpu.VMEM((tm, c), BF16)],
        compiler_params=_cp(("parallel", "parallel", "arbitrary")), name="merge",
    )(yf, yb, bonus, g, ln_g, ln_b, avg_blk, ym, pg, pg, b_gate, b_gate, p_rwkv, p_mla)


def _out_proj_kernel(a_ref, w_ref, h_ref, o_ref):
    o_ref[...] = h_ref[...] + _dot(a_ref[...], w_ref[...])


def _out_proj(merged, w_o, hp, seq, tm, tn):
    b, _, d = merged.shape
    return pl.pallas_call(
        _out_proj_kernel, grid=(b, seq // tm, d // tn),
        in_specs=[pl.BlockSpec((None, tm, d), lambda bi, i, j: (bi, i, 0)),
                  pl.BlockSpec((d, tn), lambda bi, i, j: (0, j)),
                  pl.BlockSpec((None, tm, tn), lambda bi, i, j: (bi, i, j))],
        out_specs=pl.BlockSpec((None, tm, tn), lambda bi, i, j: (bi, i, j)),
        out_shape=jax.ShapeDtypeStruct((b, seq, d), F32),
        compiler_params=_cp(("parallel", "parallel", "arbitrary")), name="out_proj")(merged, w_o, hp)


def _topk_rows(s, order, ids, k):
    vals, sel = [], []
    for _ in range(k):
        m = jnp.max(s, axis=0, keepdims=True)
        first = jnp.min(jnp.where(s == m, order, jnp.int32(2 ** 30)), axis=0, keepdims=True)
        hit = order == first
        vals.append(m)
        sel.append(first if ids is None else jnp.max(jnp.where(hit, ids, -1), axis=0, keepdims=True))
        s = jnp.where(hit, -jnp.inf, s)
    return jnp.concatenate(vals, 0), jnp.concatenate(sel, 0)


def _pair_candidates(s1, i1, s2, i2):
    k = PEER_TOPK
    assert k == 16
    tm = s1.shape[1]
    sub8 = lax.broadcasted_iota(jnp.int32, (8, tm), 0)
    sub16 = lax.broadcasted_iota(jnp.int32, (16, tm), 0)
    ninf = -jnp.inf
    vals, order, eid = [], [], []

    def add(val, o, e, valid=None):
        vals.append(val if valid is None else jnp.where(valid, val, ninf))
        order.append(o)
        eid.append(e)

    add(s1[0:1] + s2, sub16, i1[0:1] * N_KEYS + i2)
    for a, nb in ((1, 8), (2, 5), (3, 4)):
        add(s1[a:a + 1] + s2[0:8], a * k + sub8, i1[a:a + 1] * N_KEYS + i2[0:8], None if nb == 8 else sub8 < nb)
    for b, amax in ((0, 7), (1, 7), (2, 4)):
        add(s1[0:8] + s2[b:b + 1], sub8 * k + b, i1[0:8] * N_KEYS + i2[b:b + 1], (sub8 >= 4) & (sub8 <= amax))
    add(s1[8:16] + s2[0:1], (sub8 + 8) * k, i1[8:16] * N_KEYS + i2[0:1])
    return jnp.concatenate(vals, 0), jnp.concatenate(order, 0), jnp.concatenate(eid, 0)


def _route_kernel(h_ref, g_ref, wq_ref, keys_ref, e_ref, gate_ref):
    nb = _norm_rows(h_ref[...], g_ref[...]).astype(BF16)
    q = _dot(nb, wq_ref[...]).astype(BF16)
    tm = q.shape[0]
    key_rows = lax.broadcasted_iota(jnp.int32, (N_KEYS, tm), 0)
    e_rows, g_rows = [], []
    for h in range(PEER_HEADS):
        tops = []
        for half in range(2):
            gidx = 2 * h + half
            s = _dot_nt(keys_ref[gidx], q[:, gidx * N_KEYS:(gidx + 1) * N_KEYS])
            tops.append(_topk_rows(s, key_rows, None, PEER_TOPK))
        (s1, i1), (s2, i2) = tops
        best, experts = _topk_rows(*_pair_candidates(s1, i1, s2, i2), PEER_TOPK)
        ex = jnp.exp(best - best[0:1])
        g_rows.append(ex / jnp.sum(ex, axis=0, keepdims=True))
        e_rows.append(experts)
    e_ref[...] = jnp.concatenate(e_rows, 0).T
    gate_ref[...] = jnp.concatenate(g_rows, 0).T


def _route(h2, g, wq, keys, tm):
    t, d = h2.shape
    return pl.pallas_call(
        _route_kernel, grid=(t // tm,),
        in_specs=[pl.BlockSpec((tm, d), lambda i: (i, 0)),
                  pl.BlockSpec((1, d), lambda i: (0, 0)),
                  pl.BlockSpec(wq.shape, lambda i: (0, 0)),
                  pl.BlockSpec(keys.shape, lambda i: (0, 0, 0))],
        out_specs=[pl.BlockSpec((tm, PEER_SEL), lambda i: (i, 0))] * 2,
        out_shape=[jax.ShapeDtypeStruct((t, PEER_SEL), jnp.int32), jax.ShapeDtypeStruct((t, PEER_SEL), F32)],
        compiler_params=_cp(("parallel",)), name="peer_route")(h2, g, wq, keys)


def _expert_kernel(idx_ref, idxn_ref, h_ref, gate_ref, gffn_ref, gfin_ref, tab_ref, o_ref, buf_ref, sem_ref):
    i = pl.program_id(0)
    n = pl.num_programs(0)
    rows = PEER_TOK * PEER_SEL

    def row_copy(src_ref, r, dst_slot):
        return pltpu.make_async_copy(tab_ref.at[src_ref[0, 0, r]], buf_ref.at[dst_slot, pl.ds(r, 1)],
                                     sem_ref.at[dst_slot])

    @pl.when(i == 0)
    def _():
        def body(r, carry):
            row_copy(idx_ref, r, 0).start()
            return carry
        lax.fori_loop(0, rows, body, 0, unroll=8)

    def step(slot):
        @pl.when(i + 1 < n)
        def _():
            for r in range(rows):
                row_copy(idxn_ref, r, 1 - slot).start()

        pltpu.make_async_copy(buf_ref.at[slot], buf_ref.at[slot], sem_ref.at[slot]).wait()
        w = buf_ref[slot]
        u = pltpu.bitcast(w << 16, F32).astype(BF16)
        v = pltpu.bitcast(w & jnp.uint32(0xFFFF0000), F32).astype(BF16)
        h = h_ref[...]
        nb = _norm_rows(h, gffn_ref[...]).astype(BF16)
        s = _dot_nt(nb, u)
        gate = gate_ref[...]
        trow = lax.broadcasted_iota(jnp.int32, gate.shape, 0)
        gmat = jnp.concatenate([jnp.where(trow == t, gate, 0.0) for t in range(PEER_TOK)], axis=1)
        act = 0.5 * s * (1.0 + lax.erf(s * (2.0 ** -0.5))) * gmat
        o_ref[...] = _norm_rows(h + _dot(act.astype(BF16), v), gfin_ref[...])

    for parity in range(2):
        pl.when(i % 2 == parity)(functools.partial(step, parity))


def _experts(idx, h2, gate, g_ffn, g_fin, table):
    t, d = h2.shape
    nt = t // PEER_TOK
    rows = PEER_TOK * PEER_SEL
    idx3 = idx.reshape(nt, 1, rows)
    table = table.reshape(table.shape[0], 1, d)
    return pl.pallas_call(
        _expert_kernel, grid=(nt,),
        in_specs=[pl.BlockSpec((1, 1, rows), lambda i: (i, 0, 0), memory_space=pltpu.SMEM),
                  pl.BlockSpec((1, 1, rows), lambda i: (jnp.minimum(i + 1, nt - 1), 0, 0), memory_space=pltpu.SMEM),
                  pl.BlockSpec((PEER_TOK, d), lambda i: (i, 0)),
                  pl.BlockSpec((PEER_TOK, PEER_SEL), lambda i: (i, 0)),
                  pl.BlockSpec((1, d), lambda i: (0, 0)),
                  pl.BlockSpec((1, d), lambda i: (0, 0)),
                  pl.BlockSpec(memory_space=pl.ANY)],
        out_specs=pl.BlockSpec((PEER_TOK, d), lambda i: (i, 0)),
        out_shape=jax.ShapeDtypeStruct((t, d), F32),
        scratch_shapes=[pltpu.VMEM((2, rows, d), jnp.uint32), pltpu.SemaphoreType.DMA((2,))],
        compiler_params=_cp(("arbitrary",)), name="peer_experts")(idx3, idx3, h2, gate, g_ffn, g_fin, table)


def _block_const(n, blk, val):
    i = jnp.arange(n) // blk
    return jnp.where(i[:, None] == i[None, :], val, 0.0).astype(BF16)


def _layer(hp, seq, prm, table):
    b, l, d = hp.shape
    c = RWKV_DIM
    bf = lambda a: a.astype(BF16)
    w_in = prm["w_in"]
    o_q = c * 3 + GATE_LORA + 4 * LORA
    q_lora = prm["q_norm_g"].shape[0]
    kv_lora = prm["kv_norm_g"].shape[0]
    o_kr = o_q + q_lora + kv_lora
    o_g = o_kr + QK_ROPE

    def rw_cols(a):
        lead = a.shape[:-1]
        return jnp.concatenate([a[..., :3 * c], a[..., 3 * c + GATE_LORA:o_q], a[..., 3 * c:3 * c + GATE_LORA],
                                jnp.zeros(lead + (2 * QK_NOPE - GATE_LORA,), a.dtype)], -1)

    half = QK_ROPE // 2
    swap = lambda a: jnp.concatenate([a[..., half:], a[..., :half]], -1)
    padk = lambda a: jnp.concatenate([a, jnp.zeros(a.shape[:-1] + (QK_NOPE - a.shape[-1],), a.dtype)], -1)
    w_kr = w_in[:, o_kr:o_g]
    w_rw = bf(rw_cols(w_in))
    w_mla = bf(jnp.concatenate([w_in[:, o_q:o_kr], padk(w_kr), padk(swap(w_kr))], -1))
    w_gate = bf(w_in[:, o_g:])
    g_mix = prm["norm_mix_g"][None]

    tm_all = _pick(l, (384, 128))
    pr = _norm_mm(hp, g_mix, w_rw, tm_all, _pick(w_rw.shape[1], (512, 256, 128)), name="in_rwkv")
    pm = _norm_mm(hp, g_mix, w_mla, tm_all, _pick(w_mla.shape[1], (640, 128)), name="in_mla")
    pg = _norm_mm(hp, g_mix, w_gate, tm_all, 512, name="in_gate")

    zl = jnp.zeros((LORA, c), F32)
    w_lora = bf(jnp.concatenate([
        jnp.concatenate([prm["w_up"][0], zl, zl, zl], 1), jnp.concatenate([zl, prm["w_up"][1], zl, zl], 1),
        jnp.concatenate([zl, zl, prm["a_up"][0], zl], 1), jnp.concatenate([zl, zl, zl, prm["a_up"][1]], 1)], 0))
    g_up = bf(jnp.concatenate([prm["g_up"], jnp.zeros((2 * QK_NOPE - GATE_LORA, c), F32)], 0))
    ones_blk = _block_const(c, RWKV_HEAD, 1.0)
    r, v, kk, lwf, lwb, ktf, ktb, bfw, bbw, g, bonus = _rwkv_prep(
        pr, rw_cols(prm["shift_c"]), w_lora, prm["w0"], prm["a0"], g_up, prm["k_k"][None], prm["k_a"][None],
        prm["r_k"].reshape(1, c), ones_blk, seq, 128)
    t_i = np.arange(CHUNK)
    tri_f = jnp.asarray(t_i[:, None] >= t_i[None, :], BF16)
    tri_b = jnp.asarray(t_i[:, None] <= t_i[None, :], BF16)
    y_f = _scan(r, lwf, ktf, v, kk, bfw, tri_f, seq, False)
    y_b = _scan(r, lwb, ktb, v, kk, bbw, tri_b, seq, True)

    mem = np.arange(l)
    pos = np.where(mem < seq, mem + N_META, np.maximum(mem - seq - PAD_ROWS, 0)).astype(np.float32)
    inv_freq = ROPE_THETA ** (-jnp.arange(0, QK_ROPE, 2, dtype=F32) / QK_ROPE)
    ang = jnp.asarray(pos)[:, None] * inv_freq[None, :]
    cos, sin = jnp.cos(ang), jnp.sin(ang)
    z2 = jnp.zeros((l, QK_NOPE - QK_ROPE), F32)
    cs = jnp.concatenate([cos, cos, z2], 1)
    sn = jnp.concatenate([-sin, sin, z2], 1)
    qh = prm["w_uq"].reshape(q_lora, MLA_HEADS, QK_NOPE + QK_ROPE).transpose(1, 0, 2)
    wq = bf(jnp.concatenate([qh[..., :QK_NOPE], padk(qh[..., QK_NOPE:]), padk(swap(qh[..., QK_NOPE:]))], -1))
    wkv = bf(prm["w_ukv"].reshape(kv_lora, MLA_HEADS, QK_NOPE + V_HEAD).transpose(1, 0, 2))
    tm_x = _pick(seq, (512, 256, 128))
    scale = (QK_NOPE + QK_ROPE) ** -0.5
    q = _q_proj(pm, prm["q_norm_g"][None], wq, cs, sn, seq, tm_x, scale)
    k, vv = _kv_proj(pm, prm["kv_norm_g"][None], wkv, cs, sn, tm_all)
    bias = jnp.asarray(np.where(np.arange(TAIL) < PAD_ROWS, -1e30, 0.0)[None], F32)
    y_m = _attention(q, k, vv, bias, seq, tm_x)

    avg_blk = _block_const(c, RWKV_HEAD, 1.0 / RWKV_HEAD)
    merged = _merge(y_f, y_b, bonus, g, prm["ln_x_g"][None], prm["ln_x_b"][None], avg_blk, y_m, pg,
                    prm["b_gate"][None], bf(prm["p_rwkv"]), bf(prm["p_mla"]), seq, tm_x, 512)
    h2 = _out_proj(merged, bf(prm["w_o"]), hp, seq, tm_x, 512).reshape(b * seq, d)

    keys = bf(prm["peer_keys"].reshape(PEER_HEADS * 2, N_KEYS, -1))
    g_ffn = prm["norm_ffn_g"][None]
    experts, gates = _route(h2, g_ffn, bf(prm["peer_wq"]), keys, _pick(b * seq, (256, 128)))
    return experts, gates, h2, g_ffn


def kernel(x, meta_tokens, norm_mix_g, w_in, b_gate, shift_c, w_up, w0, a_up, a0, g_up, k_k, k_a, r_k,
           ln_x_g, ln_x_b, q_norm_g, w_uq, kv_norm_g, w_ukv, p_rwkv, p_mla, w_o, norm_ffn_g,
           peer_wq, peer_keys, peer_u, peer_v, final_norm_g):
    b, seq, d = x.shape
    assert norm_mix_g.shape[0] == 1, "single-layer block"
    assert seq % 128 == 0 and meta_tokens.shape[0] == N_META
    meta = jnp.broadcast_to(meta_tokens.astype(x.dtype)[None], (b, N_META, d))
    hp = jnp.concatenate([x, jnp.zeros((b, PAD_ROWS, d), x.dtype), meta], axis=1)
    prm = dict(norm_mix_g=norm_mix_g[0], w_in=w_in[0], b_gate=b_gate[0], shift_c=shift_c[0], w_up=w_up[0],
               w0=w0[0], a_up=a_up[0], a0=a0[0], g_up=g_up[0], k_k=k_k[0], k_a=k_a[0], r_k=r_k[0],
               ln_x_g=ln_x_g[0], ln_x_b=ln_x_b[0], q_norm_g=q_norm_g[0], w_uq=w_uq[0], kv_norm_g=kv_norm_g[0],
               w_ukv=w_ukv[0], p_rwkv=p_rwkv[0], p_mla=p_mla[0], w_o=w_o[0], norm_ffn_g=norm_ffn_g[0],
               peer_wq=peer_wq[0], peer_keys=peer_keys[0])
    table = _pack_tables(peer_u[0], peer_v[0])
    experts, gates, h2, g_ffn = _layer(hp, seq, prm, table)
    out = _experts(experts, h2, gates, g_ffn, final_norm_g[None], table)
    return out.reshape(b, seq, d)
```

```python
import functools
import math

import jax
import jax.numpy as jnp
import numpy as np
from jax import lax
from jax.experimental import pallas as pl
from jax.experimental.pallas import tpu as pltpu

F32 = jnp.float32
BF16 = jnp.bfloat16

N_META = 16
PAD_ROWS = 112
TAIL = PAD_ROWS + N_META
NORM_EPS = 1e-6
GN_EPS = 64e-5
ROPE_THETA = 10000.0

RWKV_HEADS = 16
RWKV_HEAD = 64
RWKV_DIM = RWKV_HEADS * RWKV_HEAD
GATE_LORA = 160
LORA = 64
CHUNK = 64

MLA_HEADS = 16
QK_NOPE = 128
QK_ROPE = 64
V_HEAD = 128
QK_PAD = 256

N_KEYS = 128
PEER_HEADS = 8
PEER_TOPK = 16
PEER_SEL = PEER_HEADS * PEER_TOPK
PEER_TOK = 8

VMEM_LIMIT = 56 * 1024 * 1024


def _cp(sem, vmem=VMEM_LIMIT):
    return pltpu.CompilerParams(dimension_semantics=sem, vmem_limit_bytes=vmem)


def _pick(n, cands):
    for c in cands:
        if n % c == 0:
            return c
    raise ValueError(f"no tile for {n}")


def _dot(a, b):
    return jnp.dot(a, b, preferred_element_type=F32)


def _dot_nt(a, b):
    return lax.dot_general(a, b, (((1,), (1,)), ((), ())), preferred_element_type=F32)


def _split_bf16(x):
    hi = x.astype(BF16)
    lo = (x - hi.astype(F32)).astype(BF16)
    return hi, lo


def _dot2(x, w):
    hi, lo = _split_bf16(x)
    return _dot(hi, w) + _dot(lo, w)


def _pack_kernel(u_ref, v_ref, o_ref):
    ub = pltpu.bitcast(u_ref[...].astype(BF16).astype(F32), jnp.uint32)
    vb = pltpu.bitcast(v_ref[...].astype(BF16).astype(F32), jnp.uint32)
    o_ref[...] = ((ub >> 16) | (vb & jnp.uint32(0xFFFF0000)))[:, None, :]


def _pack_tables(u, v):
    n, d = u.shape
    tr = _pick(n, (512, 256, 128, 8))
    spec = pl.BlockSpec((tr, d), lambda i: (i, 0))
    return pl.pallas_call(
        _pack_kernel, grid=(n // tr,), in_specs=[spec, spec],
        out_specs=pl.BlockSpec((tr, 1, d), lambda i: (i, 0, 0)),
        out_shape=jax.ShapeDtypeStruct((n, 1, d), jnp.uint32),
        compiler_params=_cp(("parallel",)), name="peer_pack")(u, v)


def _norm_rows(x, g):
    ms = jnp.mean(x * x, axis=-1, keepdims=True)
    return x * lax.rsqrt(ms + NORM_EPS) * g


def _norm_mm_kernel(x_ref, g_ref, w_ref, o_ref, nb_ref):
    @pl.when(pl.program_id(2) == 0)
    def _():
        nb_ref[...] = _norm_rows(x_ref[...], g_ref[...]).astype(BF16)

    o_ref[...] = _dot(nb_ref[...], w_ref[...]).astype(o_ref.dtype)


def _norm_mm(x, g, w, tm, tn, out_dtype=F32, name="norm_mm"):
    b, l, k = x.shape
    n = w.shape[1]
    return pl.pallas_call(
        _norm_mm_kernel, grid=(b, l // tm, n // tn),
        in_specs=[pl.BlockSpec((None, tm, k), lambda bi, i, j: (bi, i, 0)),
                  pl.BlockSpec((1, k), lambda bi, i, j: (0, 0)),
                  pl.BlockSpec((k, tn), lambda bi, i, j: (0, j))],
        out_specs=pl.BlockSpec((None, tm, tn), lambda bi, i, j: (bi, i, j)),
        out_shape=jax.ShapeDtypeStruct((b, l, n), out_dtype),
        scratch_shapes=[pltpu.VMEM((tm, k), BF16)],
        compiler_params=_cp(("parallel", "parallel", "arbitrary")), name=name)(x, g, w)


def _rwkv_prep_kernel(p_ref, pp_ref, pn_ref, sc_ref, wl_ref, w0_ref, a0_ref, gup_ref, kkw_ref, ka_ref, rk_ref,
                      ones_ref, r_ref, v_ref, kk_ref, lwf_ref, lwb_ref, ktf_ref, ktb_ref, bf_ref, bb_ref,
                      g_ref, bonus_ref, *, seq, tm):
    p = p_ref[...]
    row = lax.broadcasted_iota(jnp.int32, p.shape, 0)
    prev = jnp.where(row == 0, pp_ref[7:8, :], pltpu.roll(p, 1, 0))
    nxt = jnp.where(row == tm - 1, pn_ref[0:1, :], pltpu.roll(p, tm - 1, 0))
    xs = sc_ref[0:1, :] * prev + sc_ref[1:2, :] * p + sc_ref[2:3, :] * nxt
    c = RWKV_DIM
    r, k, v = xs[:, :c], xs[:, c:2 * c], xs[:, 2 * c:3 * c]
    lora = xs[:, 3 * c:3 * c + 4 * LORA]
    lane = lax.broadcasted_iota(jnp.int32, lora.shape, 1)
    lora = jnp.where(lane < 2 * LORA, jnp.tanh(lora), lora)
    lo = _dot(lora.astype(BF16), wl_ref[...])
    g = _dot(jax.nn.sigmoid(xs[:, 3 * c + 4 * LORA:]).astype(BF16), gup_ref[...])
    ones = ones_ref[...]
    kk = k * kkw_ref[...]
    nrm = jnp.sqrt(_dot2(kk * kk, ones))
    kk = kk / jnp.maximum(nrm, 1e-12)
    grow = pl.program_id(1) * tm + lax.broadcasted_iota(jnp.int32, v.shape, 0)
    v = jnp.where((grow < seq) | (grow >= seq + PAD_ROWS), v, 0.0)
    ka = ka_ref[...]
    kts = []
    for d, (lw_ref, kt_ref, b_ref) in enumerate(((lwf_ref, ktf_ref, bf_ref), (lwb_ref, ktb_ref, bb_ref))):
        w_log = -jax.nn.softplus(-(w0_ref[d:d + 1, :] + lo[:, d * c:(d + 1) * c])) - 0.5
        lw_ref[...] = -jnp.exp(w_log)
        iclr = jax.nn.sigmoid(a0_ref[d:d + 1, :] + lo[:, (2 + d) * c:(3 + d) * c])
        kt = k * (1.0 + (iclr - 1.0) * ka)
        kt_ref[...] = kt
        b_ref[...] = kk * iclr
        kts.append(kt)
    r_ref[...] = r
    v_ref[...] = v
    kk_ref[...] = kk
    g_ref[...] = g
    bonus_ref[...] = _dot2(r * (0.5 * (kts[0] + kts[1])) * rk_ref[...], ones) * v


def _rwkv_prep(pr, shift_c, w_lora, w0, a0, g_up, k_k, k_a, r_k, ones_blk, seq, tm):
    b, l, cols = pr.shape
    c = RWKV_DIM
    nb8 = l // 8
    row = lambda bi, i: (bi, i, 0)
    full = lambda a: pl.BlockSpec(a.shape, lambda bi, i: (0,) * a.ndim)
    out = jax.ShapeDtypeStruct((b, l, c), F32)
    ospec = pl.BlockSpec((None, tm, c), row)
    return pl.pallas_call(
        functools.partial(_rwkv_prep_kernel, seq=seq, tm=tm), grid=(b, l // tm),
        in_specs=[pl.BlockSpec((None, tm, cols), row),
                  pl.BlockSpec((None, 8, cols), lambda bi, i: (bi, (i * (tm // 8) + nb8 - 1) % nb8, 0)),
                  pl.BlockSpec((None, 8, cols), lambda bi, i: (bi, ((i + 1) * (tm // 8)) % nb8, 0)),
                  full(shift_c), full(w_lora), full(w0), full(a0), full(g_up), full(k_k), full(k_a), full(r_k),
                  full(ones_blk)],
        out_specs=[ospec] * 11, out_shape=[out] * 11,
        compiler_params=_cp(("parallel", "parallel")), name="rwkv_prep",
    )(pr, pr, pr, shift_c, w_lora, w0, a0, g_up, k_k, k_a, r_k, ones_blk)


def _scan_kernel(r_ref, lw_ref, kt_ref, v_ref, kk_ref, b_ref, tri_ref, y_ref, ht_ref, *, reverse):
    @pl.when(pl.program_id(1) == 0)
    def _():
        ht_ref[...] = jnp.zeros_like(ht_ref)

    lw = lw_ref[...]
    hi, lo = _split_bf16(lw)
    tri = tri_ref[...]
    cum = _dot(tri, hi) + _dot(tri, lo)
    tot = cum[0:1] if reverse else cum[CHUNK - 1:CHUNK]
    e_in = jnp.exp(cum)
    e_inv = jnp.exp(-cum)
    e_ex = jnp.exp(cum - lw)
    e_tot = jnp.exp(tot)
    at_all = -(kk_ref[...] * e_ex)
    rt_all = r_ref[...] * e_in
    bt_all = b_ref[...] * e_inv
    kt_all = kt_ref[...] * e_inv
    v_all = v_ref[...]

    n2 = 2 * CHUNK
    ri = lax.broadcasted_iota(jnp.int32, (n2, n2), 0)
    ci = lax.broadcasted_iota(jnp.int32, (n2, n2), 1)
    tt, ss = ri % CHUNK, ci % CHUNK
    strict = (tt < ss) if reverse else (tt > ss)
    incl = (tt <= ss) if reverse else (tt >= ss)
    bd = (ri < CHUNK) == (ci < CHUNK)
    eye = (ri == ci).astype(F32)
    head0 = lax.broadcasted_iota(jnp.int32, (CHUNK, n2), 1) < RWKV_HEAD
    cat = jnp.concatenate
    bf = lambda t: t.astype(BF16)

    pairs = range(RWKV_HEADS // 2)
    sls = [slice(p * n2, (p + 1) * n2) for p in pairs]
    h_old = [ht_ref[p] for p in pairs]
    at = [at_all[:, sl] for sl in sls]
    rt = [rt_all[:, sl] for sl in sls]
    bt = [bt_all[:, sl] for sl in sls]
    kt = [kt_all[:, sl] for sl in sls]
    v = [v_all[:, sl] for sl in sls]
    at0 = [jnp.where(head0, t, 0.0) for t in at]
    rt0 = [jnp.where(head0, t, 0.0) for t in rt]
    v0 = [jnp.where(head0, t, 0.0) for t in v]
    g0 = [_dot_nt(bf(cat([at0[p], rt0[p]], 0)), bf(cat([bt[p], kt[p]], 0))) for p in pairs]
    g1 = [_dot_nt(bf(cat([at[p] - at0[p], rt[p] - rt0[p]], 0)), bf(cat([kt[p], bt[p]], 0))) for p in pairs]
    ga = [jnp.where(strict, cat([g0[p][:CHUNK], g1[p][:CHUNK]], 0), 0.0) for p in pairs]
    gr = [jnp.where(incl, cat([g0[p][CHUNK:], g1[p][CHUNK:]], 0), 0.0) for p in pairs]
    pk = [jnp.where(bd, t, 0.0) for t in ga]
    ga_anti = [bf(ga[p] - pk[p]) for p in pairs]
    tinv = [eye + t for t in pk]
    for _ in range(int(math.log2(CHUNK)) - 1):
        pk = [_dot(bf(t), bf(t)) for t in pk]
        tinv = [tinv[p] + _dot(bf(tinv[p]), bf(pk[p])) for p in pairs]
    ht = [bf(t) for t in h_old]
    vx = [bf(cat([v[p] - v0[p], v0[p]], 0)) for p in pairs]
    ws = [_dot_nt(bf(cat([at0[p], at[p] - at0[p]], 0)), ht[p]) + _dot(ga_anti[p], vx[p]) for p in pairs]
    us = [_dot(bf(tinv[p]), bf(ws[p])) for p in pairs]
    gr_d = [jnp.where(bd, t, 0.0) for t in gr]
    ys = [_dot_nt(bf(cat([rt0[p], rt[p] - rt0[p]], 0)), ht[p])
          + _dot(bf(cat([gr_d[p], gr[p] - gr_d[p]], 1)), cat([bf(us[p]), vx[p]], 0)) for p in pairs]
    et = [e_tot[:, sl] for sl in sls]
    hn = [_dot(bf(cat([us[p][:CHUNK] + us[p][CHUNK:], v[p]], 0).T), bf(cat([bt[p] * et[p], kt[p] * et[p]], 0)))
          for p in pairs]
    for p in pairs:
        y_ref[:, sls[p]] = ys[p][:CHUNK] + ys[p][CHUNK:]
        ht_ref[p] = et[p] * h_old[p] + jnp.where(bd, hn[p], 0.0)


def _scan(r, lw, kt, v, kk, bb, tri, seq, reverse):
    b, l, c = r.shape
    nmem = l // CHUNK
    nx = seq // CHUNK
    nc = nmem
    if reverse:
        cmap = lambda bi, ci: (bi, jnp.where(ci < nx, nx - 1 - ci, nx + nmem - 1 - ci), 0)
    else:
        cmap = lambda bi, ci: (bi, (ci + nmem - 1) % nmem, 0)
    spec = pl.BlockSpec((None, CHUNK, c), cmap)
    return pl.pallas_call(
        functools.partial(_scan_kernel, reverse=reverse), grid=(b, nc),
        in_specs=[spec] * 6 + [pl.BlockSpec((CHUNK, CHUNK), lambda bi, ci: (0, 0))],
        out_specs=spec, out_shape=jax.ShapeDtypeStruct((b, l, c), F32),
        scratch_shapes=[pltpu.VMEM((RWKV_HEADS // 2, 2 * CHUNK, 2 * CHUNK), F32)],
        compiler_params=_cp(("parallel", "arbitrary")), name="wkv_bwd" if reverse else "wkv_fwd",
    )(r, lw, kt, v, kk, bb, tri)


def _q_proj_kernel(x_ref, g_ref, w_ref, cs_ref, sn_ref, o_ref, nb_ref, *, scale):
    @pl.when(pl.program_id(2) == 0)
    def _():
        nb_ref[...] = _norm_rows(x_ref[...], g_ref[...]).astype(BF16)

    acc = _dot(nb_ref[...], w_ref[...])
    o_ref[:, :QK_NOPE] = (acc[:, :QK_NOPE] * scale).astype(o_ref.dtype)
    rot = acc[:, QK_NOPE:2 * QK_NOPE] * cs_ref[...] + acc[:, 2 * QK_NOPE:] * sn_ref[...]
    o_ref[:, QK_NOPE:] = (rot * scale).astype(o_ref.dtype)


def _q_proj(pm, g, wq, cs, sn, seq, tm, scale):
    b = pm.shape[0]
    k = g.shape[1]
    return pl.pallas_call(
        functools.partial(_q_proj_kernel, scale=scale), grid=(b, seq // tm, MLA_HEADS),
        in_specs=[pl.BlockSpec((None, tm, k), lambda bi, i, h: (bi, i, 0)),
                  pl.BlockSpec((1, k), lambda bi, i, h: (0, 0)),
                  pl.BlockSpec((None, k, 3 * QK_NOPE), lambda bi, i, h: (h, 0, 0)),
                  pl.BlockSpec((tm, QK_NOPE), lambda bi, i, h: (i, 0)),
                  pl.BlockSpec((tm, QK_NOPE), lambda bi, i, h: (i, 0))],
        out_specs=pl.BlockSpec((None, None, tm, QK_PAD), lambda bi, i, h: (bi, h, i, 0)),
        out_shape=jax.ShapeDtypeStruct((b, MLA_HEADS, seq, QK_PAD), BF16),
        scratch_shapes=[pltpu.VMEM((tm, k), BF16)],
        compiler_params=_cp(("parallel", "parallel", "arbitrary")), name="mla_q")(pm, g, wq, cs, sn)


def _kv_proj_kernel(x_ref, g_ref, w_ref, ka_ref, kb_ref, cs_ref, sn_ref, k_ref, v_ref, nb_ref):
    @pl.when(pl.program_id(2) == 0)
    def _():
        nb_ref[...] = _norm_rows(x_ref[...], g_ref[...]).astype(BF16)

    acc = _dot(nb_ref[...], w_ref[...])
    k_ref[:, :QK_NOPE] = acc[:, :QK_NOPE].astype(k_ref.dtype)
    k_ref[:, QK_NOPE:] = (ka_ref[...] * cs_ref[...] + kb_ref[...] * sn_ref[...]).astype(k_ref.dtype)
    v_ref[...] = acc[:, QK_NOPE:].astype(v_ref.dtype)


def _kv_proj(pm, g, wkv, cs, sn, tm):
    b, l, _ = pm.shape
    k = g.shape[1]
    kb = k // QK_NOPE
    return pl.pallas_call(
        _kv_proj_kernel, grid=(b, l // tm, MLA_HEADS),
        in_specs=[pl.BlockSpec((None, tm, k), lambda bi, i, h: (bi, i, 1)),
                  pl.BlockSpec((1, k), lambda bi, i, h: (0, 0)),
                  pl.BlockSpec((None, k, QK_NOPE + V_HEAD), lambda bi, i, h: (h, 0, 0)),
                  pl.BlockSpec((None, tm, QK_NOPE), lambda bi, i, h: (bi, i, 2 * kb)),
                  pl.BlockSpec((None, tm, QK_NOPE), lambda bi, i, h: (bi, i, 2 * kb + 1)),
                  pl.BlockSpec((tm, QK_NOPE), lambda bi, i, h: (i, 0)),
                  pl.BlockSpec((tm, QK_NOPE), lambda bi, i, h: (i, 0))],
        out_specs=[pl.BlockSpec((None, None, tm, QK_PAD), lambda bi, i, h: (bi, h, i, 0)),
                   pl.BlockSpec((None, None, tm, V_HEAD), lambda bi, i, h: (bi, h, i, 0))],
        out_shape=[jax.ShapeDtypeStruct((b, MLA_HEADS, l, QK_PAD), BF16),
                   jax.ShapeDtypeStruct((b, MLA_HEADS, l, V_HEAD), BF16)],
        scratch_shapes=[pltpu.VMEM((tm, k), BF16)],
        compiler_params=_cp(("parallel", "parallel", "arbitrary")), name="mla_kv")(pm, g, wkv, pm, pm, cs, sn)


def _attn_kernel(q_ref, k_ref, v_ref, bias_ref, o_ref, *, seq, bk):
    q = q_ref[...]
    bq = q.shape[0]

    def update(carry, s, v):
        m, den, acc = carry
        m_new = jnp.maximum(m, jnp.max(s, axis=-1, keepdims=True))
        alpha = jnp.exp(m - m_new)
        p = jnp.exp(s - m_new)
        return (m_new, alpha * den + jnp.sum(p, axis=-1, keepdims=True), alpha * acc + _dot(p.astype(BF16), v))

    s_t = _dot_nt(q, k_ref[seq:, :]) + bias_ref[...]
    carry = update((jnp.full((bq, 1), -jnp.inf, F32), jnp.zeros((bq, 1), F32), jnp.zeros((bq, V_HEAD), F32)),
                   s_t, v_ref[seq:, :])
    for c in range(seq // bk):
        carry = update(carry, _dot_nt(q, k_ref[c * bk:(c + 1) * bk, :]), v_ref[c * bk:(c + 1) * bk, :])
    _, den, acc = carry
    o_ref[...] = (acc / den).astype(o_ref.dtype)


def _attention(q, k, v, bias, seq, bq):
    b, h, l, _ = k.shape
    return pl.pallas_call(
        functools.partial(_attn_kernel, seq=seq, bk=_pick(seq, (512, 256, 128))), grid=(b, h, seq // bq),
        in_specs=[pl.BlockSpec((None, None, bq, QK_PAD), lambda bi, hi, i: (bi, hi, i, 0)),
                  pl.BlockSpec((None, None, l, QK_PAD), lambda bi, hi, i: (bi, hi, 0, 0)),
                  pl.BlockSpec((None, None, l, V_HEAD), lambda bi, hi, i: (bi, hi, 0, 0)),
                  pl.BlockSpec((1, TAIL), lambda bi, hi, i: (0, 0))],
        out_specs=pl.BlockSpec((None, bq, V_HEAD), lambda bi, hi, i: (bi, i, hi)),
        out_shape=jax.ShapeDtypeStruct((b, seq, h * V_HEAD), BF16),
        compiler_params=_cp(("parallel", "parallel", "arbitrary")), name="mla_attn")(q, k, v, bias)


def _merge_kernel(yf_ref, yb_ref, bonus_ref, g_ref, lng_ref, lnb_ref, avg_ref, ym_ref, gr_ref, gm_ref,
                  bgr_ref, bgm_ref, pr_ref, pm_ref, o_ref, yr_ref):
    @pl.when(pl.program_id(2) == 0)
    def _():
        y = yf_ref[...] + yb_ref[...]
        avg = avg_ref[...]
        d = y - _dot2(y, avg)
        var = _dot2(d * d, avg)
        yn = d * lax.rsqrt(var + GN_EPS) * lng_ref[...] + lnb_ref[...]
        yr_ref[...] = ((yn + bonus_ref[...]) * g_ref[...]).astype(BF16)

    a_r = _dot(yr_ref[...], pr_ref[...])
    a_m = _dot(ym_ref[...], pm_ref[...])
    merged = jax.nn.sigmoid(gr_ref[...] + bgr_ref[...]) * a_r + jax.nn.sigmoid(gm_ref[...] + bgm_ref[...]) * a_m
    o_ref[...] = merged.astype(o_ref.dtype)


def _merge(yf, yb, bonus, g, ln_g, ln_b, avg_blk, ym, pg, b_gate, p_rwkv, p_mla, seq, tm, tn):
    b = yf.shape[0]
    c = RWKV_DIM
    d = p_rwkv.shape[1]
    nj = d // tn
    rowc = pl.BlockSpec((None, tm, c), lambda bi, i, j: (bi, i, 0))
    vec = pl.BlockSpec((1, c), lambda bi, i, j: (0, 0))
    return pl.pallas_call(
        _merge_kernel, grid=(b, seq // tm, nj),
        in_specs=[rowc, rowc, rowc, rowc, vec, vec,
                  pl.BlockSpec((c, c), lambda bi, i, j: (0, 0)),
                  pl.BlockSpec((None, tm, ym.shape[2]), lambda bi, i, j: (bi, i, 0)),
                  pl.BlockSpec((None, tm, tn), lambda bi, i, j: (bi, i, j)),
                  pl.BlockSpec((None, tm, tn), lambda bi, i, j: (bi, i, j + nj)),
                  pl.BlockSpec((1, tn), lambda bi, i, j: (0, j)),
                  pl.BlockSpec((1, tn), lambda bi, i, j: (0, j + nj)),
                  pl.BlockSpec((c, tn), lambda bi, i, j: (0, j)),
                  pl.BlockSpec((p_mla.shape[0], tn), lambda bi, i, j: (0, j))],
        out_specs=pl.BlockSpec((None, tm, tn), lambda bi, i, j: (bi, i, j)),
        out_shape=jax.ShapeDtypeStruct((b, seq, d), BF16),
        scratch_shapes=[pltpu.VMEM((tm, c), BF16)],
        compiler_params=_cp(("parallel", "parallel", "arbitrary")), name="merge",
    )(yf, yb, bonus, g, ln_g, ln_b, avg_blk, ym, pg, pg, b_gate, b_gate, p_rwkv, p_mla)


def _out_proj_kernel(a_ref, w_ref, h_ref, o_ref):
    o_ref[...] = h_ref[...] + _dot(a_ref[...], w_ref[...])


def _out_proj(merged, w_o, hp, seq, tm, tn):
    b, _, d = merged.shape
    return pl.pallas_call(
        _out_proj_kernel, grid=(b, seq // tm, d // tn),
        in_specs=[pl.BlockSpec((None, tm, d), lambda bi, i, j: (bi, i, 0)),
                  pl.BlockSpec((d, tn), lambda bi, i, j: (0, j)),
                  pl.BlockSpec((None, tm, tn), lambda bi, i, j: (bi, i, j))],
        out_specs=pl.BlockSpec((None, tm, tn), lambda bi, i, j: (bi, i, j)),
        out_shape=jax.ShapeDtypeStruct((b, seq, d), F32),
        compiler_params=_cp(("parallel", "parallel", "arbitrary")), name="out_proj")(merged, w_o, hp)


def _topk_rows(s, order, ids, k):
    vals, sel = [], []
    for _ in range(k):
        m = jnp.max(s, axis=0, keepdims=True)
        first = jnp.min(jnp.where(s == m, order, jnp.int32(2 ** 30)), axis=0, keepdims=True)
        hit = order == first
        vals.append(m)
        sel.append(first if ids is None else jnp.max(jnp.where(hit, ids, -1), axis=0, keepdims=True))
        s = jnp.where(hit, -jnp.inf, s)
    return jnp.concatenate(vals, 0), jnp.concatenate(sel, 0)


def _pair_candidates(s1, i1, s2, i2):
    k = PEER_TOPK
    assert k == 16
    tm = s1.shape[1]
    sub8 = lax.broadcasted_iota(jnp.int32, (8, tm), 0)
    sub16 = lax.broadcasted_iota(jnp.int32, (16, tm), 0)
    ninf = -jnp.inf
    vals, order, eid = [], [], []

    def add(val, o, e, valid=None):
        vals.append(val if valid is None else jnp.where(valid, val, ninf))
        order.append(o)
        eid.append(e)

    add(s1[0:1] + s2, sub16, i1[0:1] * N_KEYS + i2)
    for a, nb in ((1, 8), (2, 5), (3, 4)):
        add(s1[a:a + 1] + s2[0:8], a * k + sub8, i1[a:a + 1] * N_KEYS + i2[0:8], None if nb == 8 else sub8 < nb)
    for b, amax in ((0, 7), (1, 7), (2, 4)):
        add(s1[0:8] + s2[b:b + 1], sub8 * k + b, i1[0:8] * N_KEYS + i2[b:b + 1], (sub8 >= 4) & (sub8 <= amax))
    add(s1[8:16] + s2[0:1], (sub8 + 8) * k, i1[8:16] * N_KEYS + i2[0:1])
    return jnp.concatenate(vals, 0), jnp.concatenate(order, 0), jnp.concatenate(eid, 0)


def _route_kernel(h_ref, g_ref, wq_ref, keys_ref, e_ref, gate_ref):
    nb = _norm_rows(h_ref[...], g_ref[...]).astype(BF16)
    q = _dot(nb, wq_ref[...]).astype(BF16)
    tm = q.shape[0]
    key_rows = lax.broadcasted_iota(jnp.int32, (N_KEYS, tm), 0)
    e_rows, g_rows = [], []
    for h in range(PEER_HEADS):
        tops = []
        for half in range(2):
            gidx = 2 * h + half
            s = _dot_nt(keys_ref[gidx], q[:, gidx * N_KEYS:(gidx + 1) * N_KEYS])
            tops.append(_topk_rows(s, key_rows, None, PEER_TOPK))
        (s1, i1), (s2, i2) = tops
        best, experts = _topk_rows(*_pair_candidates(s1, i1, s2, i2), PEER_TOPK)
        ex = jnp.exp(best - best[0:1])
        g_rows.append(ex / jnp.sum(ex, axis=0, keepdims=True))
        e_rows.append(experts)
    e_ref[...] = jnp.concatenate(e_rows, 0).T
    gate_ref[...] = jnp.concatenate(g_rows, 0).T


def _route(h2, g, wq, keys, tm):
    t, d = h2.shape
    return pl.pallas_call(
        _route_kernel, grid=(t // tm,),
        in_specs=[pl.BlockSpec((tm, d), lambda i: (i, 0)),
                  pl.BlockSpec((1, d), lambda i: (0, 0)),
                  pl.BlockSpec(wq.shape, lambda i: (0, 0)),
                  pl.BlockSpec(keys.shape, lambda i: (0, 0, 0))],
        out_specs=[pl.BlockSpec((tm, PEER_SEL), lambda i: (i, 0))] * 2,
        out_shape=[jax.ShapeDtypeStruct((t, PEER_SEL), jnp.int32), jax.ShapeDtypeStruct((t, PEER_SEL), F32)],
        compiler_params=_cp(("parallel",)), name="peer_route")(h2, g, wq, keys)


def _expert_kernel(idx_ref, idxn_ref, h_ref, gate_ref, gffn_ref, gfin_ref, tab_ref, o_ref, buf_ref, sem_ref):
    i = pl.program_id(0)
    n = pl.num_programs(0)
    rows = PEER_TOK * PEER_SEL

    def row_copy(src_ref, r, dst_slot):
        return pltpu.make_async_copy(tab_ref.at[src_ref[0, 0, r]], buf_ref.at[dst_slot, pl.ds(r, 1)],
                                     sem_ref.at[dst_slot])

    @pl.when(i == 0)
    def _():
        def body(r, carry):
            row_copy(idx_ref, r, 0).start()
            return carry
        lax.fori_loop(0, rows, body, 0, unroll=8)

    def step(slot):
        @pl.when(i + 1 < n)
        def _():
            for r in range(rows):
                row_copy(idxn_ref, r, 1 - slot).start(priority=r % 2)

        pltpu.make_async_copy(buf_ref.at[slot], buf_ref.at[slot], sem_ref.at[slot]).wait()
        w = buf_ref[slot]
        u = pltpu.bitcast(w << 16, F32).astype(BF16)
        v = pltpu.bitcast(w & jnp.uint32(0xFFFF0000), F32).astype(BF16)
        h = h_ref[...]
        nb = _norm_rows(h, gffn_ref[...]).astype(BF16)
        s = _dot_nt(nb, u)
        gate = gate_ref[...]
        trow = lax.broadcasted_iota(jnp.int32, gate.shape, 0)
        gmat = jnp.concatenate([jnp.where(trow == t, gate, 0.0) for t in range(PEER_TOK)], axis=1)
        act = 0.5 * s * (1.0 + lax.erf(s * (2.0 ** -0.5))) * gmat
        o_ref[...] = _norm_rows(h + _dot(act.astype(BF16), v), gfin_ref[...])

    for parity in range(2):
        pl.when(i % 2 == parity)(functools.partial(step, parity))


def _experts(idx, h2, gate, g_ffn, g_fin, table):
    t, d = h2.shape
    nt = t // PEER_TOK
    rows = PEER_TOK * PEER_SEL
    idx3 = idx.reshape(nt, 1, rows)
    return pl.pallas_call(
        _expert_kernel, grid=(nt,),
        in_specs=[pl.BlockSpec((1, 1, rows), lambda i: (i, 0, 0), memory_space=pltpu.SMEM),
                  pl.BlockSpec((1, 1, rows), lambda i: (jnp.minimum(i + 1, nt - 1), 0, 0), memory_space=pltpu.SMEM),
                  pl.BlockSpec((PEER_TOK, d), lambda i: (i, 0)),
                  pl.BlockSpec((PEER_TOK, PEER_SEL), lambda i: (i, 0)),
                  pl.BlockSpec((1, d), lambda i: (0, 0)),
                  pl.BlockSpec((1, d), lambda i: (0, 0)),
                  pl.BlockSpec(memory_space=pl.ANY)],
        out_specs=pl.BlockSpec((PEER_TOK, d), lambda i: (i, 0)),
        out_shape=jax.ShapeDtypeStruct((t, d), F32),
        scratch_shapes=[pltpu.VMEM((2, rows, d), jnp.uint32), pltpu.SemaphoreType.DMA((2,))],
        compiler_params=_cp(("arbitrary",)), name="peer_experts")(idx3, idx3, h2, gate, g_ffn, g_fin, table)


def _block_const(n, blk, val):
    i = jnp.arange(n) // blk
    return jnp.where(i[:, None] == i[None, :], val, 0.0).astype(BF16)


def _layer(hp, seq, prm, table):
    b, l, d = hp.shape
    c = RWKV_DIM
    bf = lambda a: a.astype(BF16)
    w_in = prm["w_in"]
    o_q = c * 3 + GATE_LORA + 4 * LORA
    q_lora = prm["q_norm_g"].shape[0]
    kv_lora = prm["kv_norm_g"].shape[0]
    o_kr = o_q + q_lora + kv_lora
    o_g = o_kr + QK_ROPE

    def rw_cols(a):
        lead = a.shape[:-1]
        return jnp.concatenate([a[..., :3 * c], a[..., 3 * c + GATE_LORA:o_q], a[..., 3 * c:3 * c + GATE_LORA],
                                jnp.zeros(lead + (2 * QK_NOPE - GATE_LORA,), a.dtype)], -1)

    half = QK_ROPE // 2
    swap = lambda a: jnp.concatenate([a[..., half:], a[..., :half]], -1)
    padk = lambda a: jnp.concatenate([a, jnp.zeros(a.shape[:-1] + (QK_NOPE - a.shape[-1],), a.dtype)], -1)
    w_kr = w_in[:, o_kr:o_g]
    w_rw = bf(rw_cols(w_in))
    w_mla = bf(jnp.concatenate([w_in[:, o_q:o_kr], padk(w_kr), padk(swap(w_kr))], -1))
    w_gate = bf(w_in[:, o_g:])
    g_mix = prm["norm_mix_g"][None]

    tm_all = _pick(l, (384, 128))
    tm_in = _pick(l, (1408, 384, 128))
    pr = _norm_mm(hp, g_mix, w_rw, tm_in, _pick(w_rw.shape[1], (512, 256, 128)), name="in_rwkv")
    pm = _norm_mm(hp, g_mix, w_mla, tm_in, _pick(w_mla.shape[1], (640, 128)), name="in_mla")
    pg = _norm_mm(hp, g_mix, w_gate, tm_in, 512, name="in_gate")

    zl = jnp.zeros((LORA, c), F32)
    w_lora = bf(jnp.concatenate([
        jnp.concatenate([prm["w_up"][0], zl, zl, zl], 1), jnp.concatenate([zl, prm["w_up"][1], zl, zl], 1),
        jnp.concatenate([zl, zl, prm["a_up"][0], zl], 1), jnp.concatenate([zl, zl, zl, prm["a_up"][1]], 1)], 0))
    g_up = bf(jnp.concatenate([prm["g_up"], jnp.zeros((2 * QK_NOPE - GATE_LORA, c), F32)], 0))
    ones_blk = _block_const(c, RWKV_HEAD, 1.0)
    r, v, kk, lwf, lwb, ktf, ktb, bfw, bbw, g, bonus = _rwkv_prep(
        pr, rw_cols(prm["shift_c"]), w_lora, prm["w0"], prm["a0"], g_up, prm["k_k"][None], prm["k_a"][None],
        prm["r_k"].reshape(1, c), ones_blk, seq, 128)
    t_i = np.arange(CHUNK)
    tri_f = jnp.asarray(t_i[:, None] >= t_i[None, :], BF16)
    tri_b = jnp.asarray(t_i[:, None] <= t_i[None, :], BF16)
    y_f = _scan(r, lwf, ktf, v, kk, bfw, tri_f, seq, False)
    y_b = _scan(r, lwb, ktb, v, kk, bbw, tri_b, seq, True)

    mem = np.arange(l)
    pos = np.where(mem < seq, mem + N_META, np.maximum(mem - seq - PAD_ROWS, 0)).astype(np.float32)
    inv_freq = ROPE_THETA ** (-jnp.arange(0, QK_ROPE, 2, dtype=F32) / QK_ROPE)
    ang = jnp.asarray(pos)[:, None] * inv_freq[None, :]
    cos, sin = jnp.cos(ang), jnp.sin(ang)
    z2 = jnp.zeros((l, QK_NOPE - QK_ROPE), F32)
    cs = jnp.concatenate([cos, cos, z2], 1)
    sn = jnp.concatenate([-sin, sin, z2], 1)
    qh = prm["w_uq"].reshape(q_lora, MLA_HEADS, QK_NOPE + QK_ROPE).transpose(1, 0, 2)
    wq = bf(jnp.concatenate([qh[..., :QK_NOPE], padk(qh[..., QK_NOPE:]), padk(swap(qh[..., QK_NOPE:]))], -1))
    wkv = bf(prm["w_ukv"].reshape(kv_lora, MLA_HEADS, QK_NOPE + V_HEAD).transpose(1, 0, 2))
    tm_x = _pick(seq, (512, 256, 128))
    scale = (QK_NOPE + QK_ROPE) ** -0.5
    q = _q_proj(pm, prm["q_norm_g"][None], wq, cs, sn, seq, tm_x, scale)
    k, vv = _kv_proj(pm, prm["kv_norm_g"][None], wkv, cs, sn, tm_all)
    bias = jnp.asarray(np.where(np.arange(TAIL) < PAD_ROWS, -1e30, 0.0)[None], F32)
    y_m = _attention(q, k, vv, bias, seq, tm_x)

    avg_blk = _block_const(c, RWKV_HEAD, 1.0 / RWKV_HEAD)
    merged = _merge(y_f, y_b, bonus, g, prm["ln_x_g"][None], prm["ln_x_b"][None], avg_blk, y_m, pg,
                    prm["b_gate"][None], bf(prm["p_rwkv"]), bf(prm["p_mla"]), seq, tm_x, 512)
    h2 = _out_proj(merged, bf(prm["w_o"]), hp, seq, tm_x, 512).reshape(b * seq, d)

    keys = bf(prm["peer_keys"].reshape(PEER_HEADS * 2, N_KEYS, -1))
    g_ffn = prm["norm_ffn_g"][None]
    experts, gates = _route(h2, g_ffn, bf(prm["peer_wq"]), keys, _pick(b * seq, (256, 128)))
    return experts, gates, h2, g_ffn


def kernel(x, meta_tokens, norm_mix_g, w_in, b_gate, shift_c, w_up, w0, a_up, a0, g_up, k_k, k_a, r_k,
           ln_x_g, ln_x_b, q_norm_g, w_uq, kv_norm_g, w_ukv, p_rwkv, p_mla, w_o, norm_ffn_g,
           peer_wq, peer_keys, peer_u, peer_v, final_norm_g):
    b, seq, d = x.shape
    assert norm_mix_g.shape[0] == 1, "single-layer block"
    assert seq % 128 == 0 and meta_tokens.shape[0] == N_META
    meta = jnp.broadcast_to(meta_tokens.astype(x.dtype)[None], (b, N_META, d))
    hp = jnp.concatenate([x, jnp.zeros((b, PAD_ROWS, d), x.dtype), meta], axis=1)
    prm = dict(norm_mix_g=norm_mix_g[0], w_in=w_in[0], b_gate=b_gate[0], shift_c=shift_c[0], w_up=w_up[0],
               w0=w0[0], a_up=a_up[0], a0=a0[0], g_up=g_up[0], k_k=k_k[0], k_a=k_a[0], r_k=r_k[0],
               ln_x_g=ln_x_g[0], ln_x_b=ln_x_b[0], q_norm_g=q_norm_g[0], w_uq=w_uq[0], kv_norm_g=kv_norm_g[0],
               w_ukv=w_ukv[0], p_rwkv=p_rwkv[0], p_mla=p_mla[0], w_o=w_o[0], norm_ffn_g=norm_ffn_g[0],
               peer_wq=peer_wq[0], peer_keys=peer_keys[0])
    table = _pack_tables(peer_u[0], peer_v[0])
    experts, gates, h2, g_ffn = _layer(hp, seq, prm, table)
    out = _experts(experts, h2, gates, g_ffn, final_norm_g[None], table)
    return out.reshape(b, seq, d)
```

```python
import functools
import math

import jax
import jax.numpy as jnp
import numpy as np
from jax import lax
from jax.experimental import pallas as pl
from jax.experimental.pallas import tpu as pltpu

F32 = jnp.float32
BF16 = jnp.bfloat16

N_META = 16
PAD_ROWS = 112
TAIL = PAD_ROWS + N_META
NORM_EPS = 1e-6
GN_EPS = 64e-5
ROPE_THETA = 10000.0

RWKV_HEADS = 16
RWKV_HEAD = 64
RWKV_DIM = RWKV_HEADS * RWKV_HEAD
GATE_LORA = 160
LORA = 64
CHUNK = 64

MLA_HEADS = 16
QK_NOPE = 128
QK_ROPE = 64
V_HEAD = 128
QK_PAD = 256

N_KEYS = 128
PEER_HEADS = 8
PEER_TOPK = 16
PEER_SEL = PEER_HEADS * PEER_TOPK
PEER_TOK = 16

VMEM_LIMIT = 56 * 1024 * 1024


def _cp(sem, vmem=VMEM_LIMIT):
    return pltpu.CompilerParams(dimension_semantics=sem, vmem_limit_bytes=vmem)


def _pick(n, cands):
    for c in cands:
        if n % c == 0:
            return c
    raise ValueError(f"no tile for {n}")


def _dot(a, b):
    return jnp.dot(a, b, preferred_element_type=F32)


def _dot_nt(a, b):
    return lax.dot_general(a, b, (((1,), (1,)), ((), ())), preferred_element_type=F32)


def _split_bf16(x):
    hi = x.astype(BF16)
    lo = (x - hi.astype(F32)).astype(BF16)
    return hi, lo


def _dot2(x, w):
    hi, lo = _split_bf16(x)
    return _dot(hi, w) + _dot(lo, w)


def _pack_kernel(u_ref, v_ref, o_ref):
    ub = pltpu.bitcast(u_ref[...].astype(BF16).astype(F32), jnp.uint32)
    vb = pltpu.bitcast(v_ref[...].astype(BF16).astype(F32), jnp.uint32)
    o_ref[...] = ((ub >> 16) | (vb & jnp.uint32(0xFFFF0000)))[:, None, :]


def _pack_tables(u, v):
    n, d = u.shape
    tr = _pick(n, (512, 256, 128, 8))
    spec = pl.BlockSpec((tr, d), lambda i: (i, 0))
    return pl.pallas_call(
        _pack_kernel, grid=(n // tr,), in_specs=[spec, spec],
        out_specs=pl.BlockSpec((tr, 1, d), lambda i: (i, 0, 0)),
        out_shape=jax.ShapeDtypeStruct((n, 1, d), jnp.uint32),
        compiler_params=_cp(("parallel",)), name="peer_pack")(u, v)


def _norm_rows(x, g):
    ms = jnp.mean(x * x, axis=-1, keepdims=True)
    return x * lax.rsqrt(ms + NORM_EPS) * g


def _norm_mm_kernel(x_ref, g_ref, w_ref, o_ref, nb_ref):
    @pl.when(pl.program_id(2) == 0)
    def _():
        nb_ref[...] = _norm_rows(x_ref[...], g_ref[...]).astype(BF16)

    o_ref[...] = _dot(nb_ref[...], w_ref[...]).astype(o_ref.dtype)


def _norm_mm(x, g, w, tm, tn, out_dtype=F32, name="norm_mm"):
    b, l, k = x.shape
    n = w.shape[1]
    return pl.pallas_call(
        _norm_mm_kernel, grid=(b, l // tm, n // tn),
        in_specs=[pl.BlockSpec((None, tm, k), lambda bi, i, j: (bi, i, 0)),
                  pl.BlockSpec((1, k), lambda bi, i, j: (0, 0)),
                  pl.BlockSpec((k, tn), lambda bi, i, j: (0, j))],
        out_specs=pl.BlockSpec((None, tm, tn), lambda bi, i, j: (bi, i, j)),
        out_shape=jax.ShapeDtypeStruct((b, l, n), out_dtype),
        scratch_shapes=[pltpu.VMEM((tm, k), BF16)],
        compiler_params=_cp(("parallel", "parallel", "arbitrary")), name=name)(x, g, w)


def _rwkv_prep_kernel(p_ref, pp_ref, pn_ref, sc_ref, wl_ref, w0_ref, a0_ref, gup_ref, kkw_ref, ka_ref, rk_ref,
                      ones_ref, r_ref, v_ref, kk_ref, lwf_ref, lwb_ref, ktf_ref, ktb_ref, bf_ref, bb_ref,
                      g_ref, bonus_ref, *, seq, tm):
    p = p_ref[...]
    row = lax.broadcasted_iota(jnp.int32, p.shape, 0)
    prev = jnp.where(row == 0, pp_ref[7:8, :], pltpu.roll(p, 1, 0))
    nxt = jnp.where(row == tm - 1, pn_ref[0:1, :], pltpu.roll(p, tm - 1, 0))
    xs = sc_ref[0:1, :] * prev + sc_ref[1:2, :] * p + sc_ref[2:3, :] * nxt
    c = RWKV_DIM
    r, k, v = xs[:, :c], xs[:, c:2 * c], xs[:, 2 * c:3 * c]
    lora = xs[:, 3 * c:3 * c + 4 * LORA]
    lane = lax.broadcasted_iota(jnp.int32, lora.shape, 1)
    lora = jnp.where(lane < 2 * LORA, jnp.tanh(lora), lora)
    lo = _dot(lora.astype(BF16), wl_ref[...])
    g = _dot(jax.nn.sigmoid(xs[:, 3 * c + 4 * LORA:]).astype(BF16), gup_ref[...])
    ones = ones_ref[...]
    kk = k * kkw_ref[...]
    nrm = jnp.sqrt(_dot2(kk * kk, ones))
    kk = kk / jnp.maximum(nrm, 1e-12)
    grow = pl.program_id(1) * tm + lax.broadcasted_iota(jnp.int32, v.shape, 0)
    v = jnp.where((grow < seq) | (grow >= seq + PAD_ROWS), v, 0.0)
    ka = ka_ref[...]
    kts = []
    for d, (lw_ref, kt_ref, b_ref) in enumerate(((lwf_ref, ktf_ref, bf_ref), (lwb_ref, ktb_ref, bb_ref))):
        w_log = -jax.nn.softplus(-(w0_ref[d:d + 1, :] + lo[:, d * c:(d + 1) * c])) - 0.5
        lw_ref[...] = -jnp.exp(w_log)
        iclr = jax.nn.sigmoid(a0_ref[d:d + 1, :] + lo[:, (2 + d) * c:(3 + d) * c])
        kt = k * (1.0 + (iclr - 1.0) * ka)
        kt_ref[...] = kt
        b_ref[...] = kk * iclr
        kts.append(kt)
    r_ref[...] = r
    v_ref[...] = v
    kk_ref[...] = kk
    g_ref[...] = g
    bonus_ref[...] = _dot2(r * (0.5 * (kts[0] + kts[1])) * rk_ref[...], ones) * v


def _rwkv_prep(pr, shift_c, w_lora, w0, a0, g_up, k_k, k_a, r_k, ones_blk, seq, tm):
    b, l, cols = pr.shape
    c = RWKV_DIM
    nb8 = l // 8
    row = lambda bi, i: (bi, i, 0)
    full = lambda a: pl.BlockSpec(a.shape, lambda bi, i: (0,) * a.ndim)
    out = jax.ShapeDtypeStruct((b, l, c), F32)
    ospec = pl.BlockSpec((None, tm, c), row)
    return pl.pallas_call(
        functools.partial(_rwkv_prep_kernel, seq=seq, tm=tm), grid=(b, l // tm),
        in_specs=[pl.BlockSpec((None, tm, cols), row),
                  pl.BlockSpec((None, 8, cols), lambda bi, i: (bi, (i * (tm // 8) + nb8 - 1) % nb8, 0)),
                  pl.BlockSpec((None, 8, cols), lambda bi, i: (bi, ((i + 1) * (tm // 8)) % nb8, 0)),
                  full(shift_c), full(w_lora), full(w0), full(a0), full(g_up), full(k_k), full(k_a), full(r_k),
                  full(ones_blk)],
        out_specs=[ospec] * 11, out_shape=[out] * 11,
        compiler_params=_cp(("parallel", "parallel")), name="rwkv_prep",
    )(pr, pr, pr, shift_c, w_lora, w0, a0, g_up, k_k, k_a, r_k, ones_blk)


def _scan_kernel(r_ref, lw_ref, kt_ref, v_ref, kk_ref, b_ref, tri_ref, y_ref, ht_ref, *, reverse):
    @pl.when(pl.program_id(1) == 0)
    def _():
        ht_ref[...] = jnp.zeros_like(ht_ref)

    lw = lw_ref[...]
    hi, lo = _split_bf16(lw)
    tri = tri_ref[...]
    cum = _dot(tri, hi) + _dot(tri, lo)
    tot = cum[0:1] if reverse else cum[CHUNK - 1:CHUNK]
    e_in = jnp.exp(cum)
    e_inv = jnp.exp(-cum)
    e_ex = jnp.exp(cum - lw)
    e_tot = jnp.exp(tot)
    at_all = -(kk_ref[...] * e_ex)
    rt_all = r_ref[...] * e_in
    bt_all = b_ref[...] * e_inv
    kt_all = kt_ref[...] * e_inv
    v_all = v_ref[...]

    n2 = 2 * CHUNK
    ri = lax.broadcasted_iota(jnp.int32, (n2, n2), 0)
    ci = lax.broadcasted_iota(jnp.int32, (n2, n2), 1)
    tt, ss = ri % CHUNK, ci % CHUNK
    strict = (tt < ss) if reverse else (tt > ss)
    incl = (tt <= ss) if reverse else (tt >= ss)
    bd = (ri < CHUNK) == (ci < CHUNK)
    eye = (ri == ci).astype(F32)
    head0 = lax.broadcasted_iota(jnp.int32, (CHUNK, n2), 1) < RWKV_HEAD
    cat = jnp.concatenate
    bf = lambda t: t.astype(BF16)

    pairs = range(RWKV_HEADS // 2)
    sls = [slice(p * n2, (p + 1) * n2) for p in pairs]
    h_old = [ht_ref[p] for p in pairs]
    at = [at_all[:, sl] for sl in sls]
    rt = [rt_all[:, sl] for sl in sls]
    bt = [bt_all[:, sl] for sl in sls]
    kt = [kt_all[:, sl] for sl in sls]
    v = [v_all[:, sl] for sl in sls]
    at0 = [jnp.where(head0, t, 0.0) for t in at]
    rt0 = [jnp.where(head0, t, 0.0) for t in rt]
    v0 = [jnp.where(head0, t, 0.0) for t in v]
    g0 = [_dot_nt(bf(cat([at0[p], rt0[p]], 0)), bf(cat([bt[p], kt[p]], 0))) for p in pairs]
    g1 = [_dot_nt(bf(cat([at[p] - at0[p], rt[p] - rt0[p]], 0)), bf(cat([kt[p], bt[p]], 0))) for p in pairs]
    ga = [jnp.where(strict, cat([g0[p][:CHUNK], g1[p][:CHUNK]], 0), 0.0) for p in pairs]
    gr = [jnp.where(incl, cat([g0[p][CHUNK:], g1[p][CHUNK:]], 0), 0.0) for p in pairs]
    pk = [jnp.where(bd, t, 0.0) for t in ga]
    ga_anti = [bf(ga[p] - pk[p]) for p in pairs]
    tinv = [eye + t for t in pk]
    for _ in range(int(math.log2(CHUNK)) - 1):
        pk = [_dot(bf(t), bf(t)) for t in pk]
        tinv = [tinv[p] + _dot(bf(tinv[p]), bf(pk[p])) for p in pairs]
    ht = [bf(t) for t in h_old]
    vx = [bf(cat([v[p] - v0[p], v0[p]], 0)) for p in pairs]
    ws = [_dot_nt(bf(cat([at0[p], at[p] - at0[p]], 0)), ht[p]) + _dot(ga_anti[p], vx[p]) for p in pairs]
    us = [_dot(bf(tinv[p]), bf(ws[p])) for p in pairs]
    gr_d = [jnp.where(bd, t, 0.0) for t in gr]
    ys = [_dot_nt(bf(cat([rt0[p], rt[p] - rt0[p]], 0)), ht[p])
          + _dot(bf(cat([gr_d[p], gr[p] - gr_d[p]], 1)), cat([bf(us[p]), vx[p]], 0)) for p in pairs]
    et = [e_tot[:, sl] for sl in sls]
    hn = [_dot(bf(cat([us[p][:CHUNK] + us[p][CHUNK:], v[p]], 0).T), bf(cat([bt[p] * et[p], kt[p] * et[p]], 0)))
          for p in pairs]
    for p in pairs:
        y_ref[:, sls[p]] = ys[p][:CHUNK] + ys[p][CHUNK:]
        ht_ref[p] = et[p] * h_old[p] + jnp.where(bd, hn[p], 0.0)


def _scan(r, lw, kt, v, kk, bb, tri, seq, reverse):
    b, l, c = r.shape
    nmem = l // CHUNK
    nx = seq // CHUNK
    nc = nmem
    if reverse:
        cmap = lambda bi, ci: (bi, jnp.where(ci < nx, nx - 1 - ci, nx + nmem - 1 - ci), 0)
    else:
        cmap = lambda bi, ci: (bi, (ci + nmem - 1) % nmem, 0)
    spec = pl.BlockSpec((None, CHUNK, c), cmap)
    return pl.pallas_call(
        functools.partial(_scan_kernel, reverse=reverse), grid=(b, nc),
        in_specs=[spec] * 6 + [pl.BlockSpec((CHUNK, CHUNK), lambda bi, ci: (0, 0))],
        out_specs=spec, out_shape=jax.ShapeDtypeStruct((b, l, c), F32),
        scratch_shapes=[pltpu.VMEM((RWKV_HEADS // 2, 2 * CHUNK, 2 * CHUNK), F32)],
        compiler_params=_cp(("parallel", "arbitrary")), name="wkv_bwd" if reverse else "wkv_fwd",
    )(r, lw, kt, v, kk, bb, tri)


def _q_proj_kernel(x_ref, g_ref, w_ref, cs_ref, sn_ref, o_ref, *, scale):
    acc = _dot(_norm_rows(x_ref[...], g_ref[...]).astype(BF16), w_ref[...])
    cs = cs_ref[...] * scale
    sn = sn_ref[...] * scale
    hw = MLA_HEADS * QK_NOPE
    for h in range(MLA_HEADS):
        lo = h * QK_NOPE
        o_ref[h, :, :QK_NOPE] = (acc[:, lo:lo + QK_NOPE] * scale).astype(o_ref.dtype)
        rot = acc[:, hw + lo:hw + lo + QK_NOPE] * cs + acc[:, 2 * hw + lo:2 * hw + lo + QK_NOPE] * sn
        o_ref[h, :, QK_NOPE:] = rot.astype(o_ref.dtype)


def _q_proj(pm, g, wq, cs, sn, seq, tm, scale):
    b = pm.shape[0]
    k = g.shape[1]
    return pl.pallas_call(
        functools.partial(_q_proj_kernel, scale=scale), grid=(b, seq // tm),
        in_specs=[pl.BlockSpec((None, tm, k), lambda bi, i: (bi, i, 0)),
                  pl.BlockSpec((1, k), lambda bi, i: (0, 0)),
                  pl.BlockSpec(wq.shape, lambda bi, i: (0, 0)),
                  pl.BlockSpec((tm, QK_NOPE), lambda bi, i: (i, 0)),
                  pl.BlockSpec((tm, QK_NOPE), lambda bi, i: (i, 0))],
        out_specs=pl.BlockSpec((None, MLA_HEADS, tm, QK_PAD), lambda bi, i: (bi, 0, i, 0)),
        out_shape=jax.ShapeDtypeStruct((b, MLA_HEADS, seq, QK_PAD), BF16),
        compiler_params=_cp(("parallel", "parallel")), name="mla_q")(pm, g, wq, cs, sn)


def _kv_proj_kernel(x_ref, g_ref, w_ref, ka_ref, kb_ref, cs_ref, sn_ref, k_ref, v_ref):
    acc = _dot(_norm_rows(x_ref[...], g_ref[...]).astype(BF16), w_ref[...])
    rope = (ka_ref[...] * cs_ref[...] + kb_ref[...] * sn_ref[...]).astype(k_ref.dtype)
    hw = MLA_HEADS * QK_NOPE
    for h in range(MLA_HEADS):
        lo = h * QK_NOPE
        k_ref[h, :, :QK_NOPE] = acc[:, lo:lo + QK_NOPE].astype(k_ref.dtype)
        k_ref[h, :, QK_NOPE:] = rope
        v_ref[h] = acc[:, hw + lo:hw + lo + V_HEAD].astype(v_ref.dtype)


def _kv_proj(pm, g, wkv, cs, sn, tm):
    b, l, _ = pm.shape
    k = g.shape[1]
    kb = k // QK_NOPE
    return pl.pallas_call(
        _kv_proj_kernel, grid=(b, l // tm),
        in_specs=[pl.BlockSpec((None, tm, k), lambda bi, i: (bi, i, 1)),
                  pl.BlockSpec((1, k), lambda bi, i: (0, 0)),
                  pl.BlockSpec(wkv.shape, lambda bi, i: (0, 0)),
                  pl.BlockSpec((None, tm, QK_NOPE), lambda bi, i: (bi, i, 2 * kb)),
                  pl.BlockSpec((None, tm, QK_NOPE), lambda bi, i: (bi, i, 2 * kb + 1)),
                  pl.BlockSpec((tm, QK_NOPE), lambda bi, i: (i, 0)),
                  pl.BlockSpec((tm, QK_NOPE), lambda bi, i: (i, 0))],
        out_specs=[pl.BlockSpec((None, MLA_HEADS, tm, QK_PAD), lambda bi, i: (bi, 0, i, 0)),
                   pl.BlockSpec((None, MLA_HEADS, tm, V_HEAD), lambda bi, i: (bi, 0, i, 0))],
        out_shape=[jax.ShapeDtypeStruct((b, MLA_HEADS, l, QK_PAD), BF16),
                   jax.ShapeDtypeStruct((b, MLA_HEADS, l, V_HEAD), BF16)],
        compiler_params=_cp(("parallel", "parallel")), name="mla_kv")(pm, g, wkv, pm, pm, cs, sn)


def _attn_kernel(q_ref, k_ref, v_ref, bias_ref, o_ref, *, seq, bk):
    q = q_ref[...]
    bq = q.shape[0]

    def update(carry, s, v):
        m, den, acc = carry
        m_new = jnp.maximum(m, jnp.max(s, axis=-1, keepdims=True))
        alpha = jnp.exp2(m - m_new)
        p = jnp.exp2(s - m_new)
        return (m_new, alpha * den + jnp.sum(p, axis=-1, keepdims=True), alpha * acc + _dot(p.astype(BF16), v))

    s_t = _dot_nt(q, k_ref[seq:, :]) + bias_ref[...]
    carry = update((jnp.full((bq, 1), -jnp.inf, F32), jnp.zeros((bq, 1), F32), jnp.zeros((bq, V_HEAD), F32)),
                   s_t, v_ref[seq:, :])
    for c in range(seq // bk):
        carry = update(carry, _dot_nt(q, k_ref[c * bk:(c + 1) * bk, :]), v_ref[c * bk:(c + 1) * bk, :])
    _, den, acc = carry
    o_ref[...] = (acc / den).astype(o_ref.dtype)


def _attention(q, k, v, bias, seq, bq):
    b, h, l, _ = k.shape
    return pl.pallas_call(
        functools.partial(_attn_kernel, seq=seq, bk=_pick(seq, (512, 256, 128))), grid=(b, h, seq // bq),
        in_specs=[pl.BlockSpec((None, None, bq, QK_PAD), lambda bi, hi, i: (bi, hi, i, 0)),
                  pl.BlockSpec((None, None, l, QK_PAD), lambda bi, hi, i: (bi, hi, 0, 0)),
                  pl.BlockSpec((None, None, l, V_HEAD), lambda bi, hi, i: (bi, hi, 0, 0)),
                  pl.BlockSpec((1, TAIL), lambda bi, hi, i: (0, 0))],
        out_specs=pl.BlockSpec((None, bq, V_HEAD), lambda bi, hi, i: (bi, i, hi)),
        out_shape=jax.ShapeDtypeStruct((b, seq, h * V_HEAD), BF16),
        compiler_params=_cp(("parallel", "parallel", "arbitrary")), name="mla_attn")(q, k, v, bias)


def _merge_kernel(yf_ref, yb_ref, bonus_ref, g_ref, lng_ref, lnb_ref, avg_ref, ym_ref, gr_ref, gm_ref,
                  bgr_ref, bgm_ref, pr_ref, pm_ref, o_ref, yr_ref):
    @pl.when(pl.program_id(2) == 0)
    def _():
        y = yf_ref[...] + yb_ref[...]
        avg = avg_ref[...]
        d = y - _dot2(y, avg)
        var = _dot2(d * d, avg)
        yn = d * lax.rsqrt(var + GN_EPS) * lng_ref[...] + lnb_ref[...]
        yr_ref[...] = ((yn + bonus_ref[...]) * g_ref[...]).astype(BF16)

    a_r = _dot(yr_ref[...], pr_ref[...])
    a_m = _dot(ym_ref[...], pm_ref[...])
    merged = jax.nn.sigmoid(gr_ref[...] + bgr_ref[...]) * a_r + jax.nn.sigmoid(gm_ref[...] + bgm_ref[...]) * a_m
    o_ref[...] = merged.astype(o_ref.dtype)


def _merge(yf, yb, bonus, g, ln_g, ln_b, avg_blk, ym, pg, b_gate, p_rwkv, p_mla, seq, tm, tn):
    b = yf.shape[0]
    c = RWKV_DIM
    d = p_rwkv.shape[1]
    nj = d // tn
    rowc = pl.BlockSpec((None, tm, c), lambda bi, i, j: (bi, i, 0))
    vec = pl.BlockSpec((1, c), lambda bi, i, j: (0, 0))
    return pl.pallas_call(
        _merge_kernel, grid=(b, seq // tm, nj),
        in_specs=[rowc, rowc, rowc, rowc, vec, vec,
                  pl.BlockSpec((c, c), lambda bi, i, j: (0, 0)),
                  pl.BlockSpec((None, tm, ym.shape[2]), lambda bi, i, j: (bi, i, 0)),
                  pl.BlockSpec((None, tm, tn), lambda bi, i, j: (bi, i, j)),
                  pl.BlockSpec((None, tm, tn), lambda bi, i, j: (bi, i, j + nj)),
                  pl.BlockSpec((1, tn), lambda bi, i, j: (0, j)),
                  pl.BlockSpec((1, tn), lambda bi, i, j: (0, j + nj)),
                  pl.BlockSpec((c, tn), lambda bi, i, j: (0, j)),
                  pl.BlockSpec((p_mla.shape[0], tn), lambda bi, i, j: (0, j))],
        out_specs=pl.BlockSpec((None, tm, tn), lambda bi, i, j: (bi, i, j)),
        out_shape=jax.ShapeDtypeStruct((b, seq, d), BF16),
        scratch_shapes=[pltpu.VMEM((tm, c), BF16)],
        compiler_params=_cp(("parallel", "parallel", "arbitrary")), name="merge",
    )(yf, yb, bonus, g, ln_g, ln_b, avg_blk, ym, pg, pg, b_gate, b_gate, p_rwkv, p_mla)


def _out_proj_kernel(a_ref, w_ref, h_ref, o_ref):
    o_ref[...] = h_ref[...] + _dot(a_ref[...], w_ref[...])


def _out_proj(merged, w_o, hp, seq, tm, tn):
    b, _, d = merged.shape
    return pl.pallas_call(
        _out_proj_kernel, grid=(b, seq // tm, d // tn),
        in_specs=[pl.BlockSpec((None, tm, d), lambda bi, i, j: (bi, i, 0)),
                  pl.BlockSpec((d, tn), lambda bi, i, j: (0, j)),
                  pl.BlockSpec((None, tm, tn), lambda bi, i, j: (bi, i, j))],
        out_specs=pl.BlockSpec((None, tm, tn), lambda bi, i, j: (bi, i, j)),
        out_shape=jax.ShapeDtypeStruct((b, seq, d), F32),
        compiler_params=_cp(("parallel", "parallel", "arbitrary")), name="out_proj")(merged, w_o, hp)


def _topk_rows(s, order, ids, k):
    vals, sel = [], []
    for _ in range(k):
        m = jnp.max(s, axis=0, keepdims=True)
        first = jnp.min(jnp.where(s == m, order, jnp.int32(2 ** 30)), axis=0, keepdims=True)
        hit = order == first
        vals.append(m)
        sel.append(first if ids is None else jnp.max(jnp.where(hit, ids, -1), axis=0, keepdims=True))
        s = jnp.where(hit, -jnp.inf, s)
    return jnp.concatenate(vals, 0), jnp.concatenate(sel, 0)


def _pair_candidates(s1, i1, s2, i2):
    k = PEER_TOPK
    assert k == 16
    tm = s1.shape[1]
    sub8 = lax.broadcasted_iota(jnp.int32, (8, tm), 0)
    sub16 = lax.broadcasted_iota(jnp.int32, (16, tm), 0)
    ninf = -jnp.inf
    vals, order, eid = [], [], []

    def add(val, o, e, valid=None):
        vals.append(val if valid is None else jnp.where(valid, val, ninf))
        order.append(o)
        eid.append(e)

    add(s1[0:1] + s2, sub16, i1[0:1] * N_KEYS + i2)
    for a, nb in ((1, 8), (2, 5), (3, 4)):
        add(s1[a:a + 1] + s2[0:8], a * k + sub8, i1[a:a + 1] * N_KEYS + i2[0:8], None if nb == 8 else sub8 < nb)
    for b, amax in ((0, 7), (1, 7), (2, 4)):
        add(s1[0:8] + s2[b:b + 1], sub8 * k + b, i1[0:8] * N_KEYS + i2[b:b + 1], (sub8 >= 4) & (sub8 <= amax))
    add(s1[8:16] + s2[0:1], (sub8 + 8) * k, i1[8:16] * N_KEYS + i2[0:1])
    return jnp.concatenate(vals, 0), jnp.concatenate(order, 0), jnp.concatenate(eid, 0)


def _route_kernel(h_ref, g_ref, wq_ref, keys_ref, e_ref, gate_ref):
    nb = _norm_rows(h_ref[...], g_ref[...]).astype(BF16)
    q = _dot(nb, wq_ref[...]).astype(BF16)
    tm = q.shape[0]
    key_rows = lax.broadcasted_iota(jnp.int32, (N_KEYS, tm), 0)
    e_rows, g_rows = [], []
    for h in range(PEER_HEADS):
        tops = []
        for half in range(2):
            gidx = 2 * h + half
            s = _dot_nt(keys_ref[gidx], q[:, gidx * N_KEYS:(gidx + 1) * N_KEYS])
            tops.append(_topk_rows(s, key_rows, None, PEER_TOPK))
        (s1, i1), (s2, i2) = tops
        best, experts = _topk_rows(*_pair_candidates(s1, i1, s2, i2), PEER_TOPK)
        ex = jnp.exp(best - best[0:1])
        g_rows.append(ex / jnp.sum(ex, axis=0, keepdims=True))
        e_rows.append(experts)
    e_ref[...] = jnp.concatenate(e_rows, 0).T
    gate_ref[...] = jnp.concatenate(g_rows, 0).T


def _route(h2, g, wq, keys, tm):
    t, d = h2.shape
    return pl.pallas_call(
        _route_kernel, grid=(t // tm,),
        in_specs=[pl.BlockSpec((tm, d), lambda i: (i, 0)),
                  pl.BlockSpec((1, d), lambda i: (0, 0)),
                  pl.BlockSpec(wq.shape, lambda i: (0, 0)),
                  pl.BlockSpec(keys.shape, lambda i: (0, 0, 0))],
        out_specs=[pl.BlockSpec((tm, PEER_SEL), lambda i: (i, 0))] * 2,
        out_shape=[jax.ShapeDtypeStruct((t, PEER_SEL), jnp.int32), jax.ShapeDtypeStruct((t, PEER_SEL), F32)],
        compiler_params=_cp(("parallel",)), name="peer_route")(h2, g, wq, keys)


def _expert_kernel(idx_ref, idxn_ref, h_ref, gate_ref, gffn_ref, gfin_ref, tab_ref, o_ref, buf_ref, sem_ref):
    i = pl.program_id(0)
    n = pl.num_programs(0)
    rows = PEER_TOK * PEER_SEL

    def row_copy(src_ref, r, dst_slot):
        return pltpu.make_async_copy(tab_ref.at[src_ref[0, 0, r]], buf_ref.at[dst_slot, pl.ds(r, 1)],
                                     sem_ref.at[dst_slot])

    @pl.when(i == 0)
    def _():
        def body(r, carry):
            row_copy(idx_ref, r, 0).start()
            return carry
        lax.fori_loop(0, rows, body, 0, unroll=8)

    def step(slot):
        @pl.when(i + 1 < n)
        def _():
            for r in range(rows):
                row_copy(idxn_ref, r, 1 - slot).start(priority=r % 2)

        pltpu.make_async_copy(buf_ref.at[slot], buf_ref.at[slot], sem_ref.at[slot]).wait()
        w = buf_ref[slot]
        u = pltpu.bitcast(w << 16, F32).astype(BF16)
        v = pltpu.bitcast(w & jnp.uint32(0xFFFF0000), F32).astype(BF16)
        h = h_ref[...]
        nb = _norm_rows(h, gffn_ref[...]).astype(BF16)
        s = _dot_nt(nb, u)
        gate = gate_ref[...]
        trow = lax.broadcasted_iota(jnp.int32, gate.shape, 0)
        gmat = jnp.concatenate([jnp.where(trow == t, gate, 0.0) for t in range(PEER_TOK)], axis=1)
        act = 0.5 * s * (1.0 + lax.erf(s * (2.0 ** -0.5))) * gmat
        o_ref[...] = _norm_rows(h + _dot(act.astype(BF16), v), gfin_ref[...])

    for parity in range(2):
        pl.when(i % 2 == parity)(functools.partial(step, parity))


def _experts(idx, h2, gate, g_ffn, g_fin, table):
    t, d = h2.shape
    nt = t // PEER_TOK
    rows = PEER_TOK * PEER_SEL
    idx3 = idx.reshape(nt, 1, rows)
    return pl.pallas_call(
        _expert_kernel, grid=(nt,),
        in_specs=[pl.BlockSpec((1, 1, rows), lambda i: (i, 0, 0), memory_space=pltpu.SMEM),
                  pl.BlockSpec((1, 1, rows), lambda i: (jnp.minimum(i + 1, nt - 1), 0, 0), memory_space=pltpu.SMEM),
                  pl.BlockSpec((PEER_TOK, d), lambda i: (i, 0)),
                  pl.BlockSpec((PEER_TOK, PEER_SEL), lambda i: (i, 0)),
                  pl.BlockSpec((1, d), lambda i: (0, 0)),
                  pl.BlockSpec((1, d), lambda i: (0, 0)),
                  pl.BlockSpec(memory_space=pl.ANY)],
        out_specs=pl.BlockSpec((PEER_TOK, d), lambda i: (i, 0)),
        out_shape=jax.ShapeDtypeStruct((t, d), F32),
        scratch_shapes=[pltpu.VMEM((2, rows, d), jnp.uint32), pltpu.SemaphoreType.DMA((2,))],
        compiler_params=_cp(("arbitrary",)), name="peer_experts")(idx3, idx3, h2, gate, g_ffn, g_fin, table)


def _block_const(n, blk, val):
    i = jnp.arange(n) // blk
    return jnp.where(i[:, None] == i[None, :], val, 0.0).astype(BF16)


def _layer(hp, seq, prm, table):
    b, l, d = hp.shape
    c = RWKV_DIM
    bf = lambda a: a.astype(BF16)
    w_in = prm["w_in"]
    o_q = c * 3 + GATE_LORA + 4 * LORA
    q_lora = prm["q_norm_g"].shape[0]
    kv_lora = prm["kv_norm_g"].shape[0]
    o_kr = o_q + q_lora + kv_lora
    o_g = o_kr + QK_ROPE

    def rw_cols(a):
        lead = a.shape[:-1]
        return jnp.concatenate([a[..., :3 * c], a[..., 3 * c + GATE_LORA:o_q], a[..., 3 * c:3 * c + GATE_LORA],
                                jnp.zeros(lead + (2 * QK_NOPE - GATE_LORA,), a.dtype)], -1)

    half = QK_ROPE // 2
    swap = lambda a: jnp.concatenate([a[..., half:], a[..., :half]], -1)
    padk = lambda a: jnp.concatenate([a, jnp.zeros(a.shape[:-1] + (QK_NOPE - a.shape[-1],), a.dtype)], -1)
    w_kr = w_in[:, o_kr:o_g]
    w_rw = bf(rw_cols(w_in))
    w_mla = bf(jnp.concatenate([w_in[:, o_q:o_kr], padk(w_kr), padk(swap(w_kr))], -1))
    w_gate = bf(w_in[:, o_g:])
    g_mix = prm["norm_mix_g"][None]

    tm_all = _pick(l, (384, 128))
    tm_in = _pick(l, (1408, 384, 128))
    pr = _norm_mm(hp, g_mix, w_rw, tm_in, _pick(w_rw.shape[1], (512, 256, 128)), name="in_rwkv")
    pm = _norm_mm(hp, g_mix, w_mla, tm_in, _pick(w_mla.shape[1], (640, 128)), name="in_mla")
    pg = _norm_mm(hp, g_mix, w_gate, tm_in, 512, name="in_gate")

    zl = jnp.zeros((LORA, c), F32)
    w_lora = bf(jnp.concatenate([
        jnp.concatenate([prm["w_up"][0], zl, zl, zl], 1), jnp.concatenate([zl, prm["w_up"][1], zl, zl], 1),
        jnp.concatenate([zl, zl, prm["a_up"][0], zl], 1), jnp.concatenate([zl, zl, zl, prm["a_up"][1]], 1)], 0))
    g_up = bf(jnp.concatenate([prm["g_up"], jnp.zeros((2 * QK_NOPE - GATE_LORA, c), F32)], 0))
    ones_blk = _block_const(c, RWKV_HEAD, 1.0)
    r, v, kk, lwf, lwb, ktf, ktb, bfw, bbw, g, bonus = _rwkv_prep(
        pr, rw_cols(prm["shift_c"]), w_lora, prm["w0"], prm["a0"], g_up, prm["k_k"][None], prm["k_a"][None],
        prm["r_k"].reshape(1, c), ones_blk, seq, 128)
    t_i = np.arange(CHUNK)
    tri_f = jnp.asarray(t_i[:, None] >= t_i[None, :], BF16)
    tri_b = jnp.asarray(t_i[:, None] <= t_i[None, :], BF16)
    y_f = _scan(r, lwf, ktf, v, kk, bfw, tri_f, seq, False)
    y_b = _scan(r, lwb, ktb, v, kk, bbw, tri_b, seq, True)

    mem = np.arange(l)
    pos = np.where(mem < seq, mem + N_META, np.maximum(mem - seq - PAD_ROWS, 0)).astype(np.float32)
    inv_freq = ROPE_THETA ** (-jnp.arange(0, QK_ROPE, 2, dtype=F32) / QK_ROPE)
    ang = jnp.asarray(pos)[:, None] * inv_freq[None, :]
    cos, sin = jnp.cos(ang), jnp.sin(ang)
    z2 = jnp.zeros((l, QK_NOPE - QK_ROPE), F32)
    cs = jnp.concatenate([cos, cos, z2], 1)
    sn = jnp.concatenate([-sin, sin, z2], 1)
    qh = prm["w_uq"].reshape(q_lora, MLA_HEADS, QK_NOPE + QK_ROPE)
    heads = lambda a: a.reshape(a.shape[0], -1)
    wq = bf(jnp.concatenate([heads(qh[..., :QK_NOPE]), heads(padk(qh[..., QK_NOPE:])),
                             heads(padk(swap(qh[..., QK_NOPE:])))], -1))
    kvh = prm["w_ukv"].reshape(kv_lora, MLA_HEADS, QK_NOPE + V_HEAD)
    wkv = bf(jnp.concatenate([heads(kvh[..., :QK_NOPE]), heads(kvh[..., QK_NOPE:])], -1))
    tm_x = _pick(seq, (512, 256, 128))
    scale = (QK_NOPE + QK_ROPE) ** -0.5 * math.log2(math.e)
    q = _q_proj(pm, prm["q_norm_g"][None], wq, cs, sn, seq, tm_x, scale)
    k, vv = _kv_proj(pm, prm["kv_norm_g"][None], wkv, cs, sn, tm_all)
    bias = jnp.asarray(np.where(np.arange(TAIL) < PAD_ROWS, -1e30, 0.0)[None], F32)
    y_m = _attention(q, k, vv, bias, seq, tm_x)

    avg_blk = _block_const(c, RWKV_HEAD, 1.0 / RWKV_HEAD)
    merged = _merge(y_f, y_b, bonus, g, prm["ln_x_g"][None], prm["ln_x_b"][None], avg_blk, y_m, pg,
                    prm["b_gate"][None], bf(prm["p_rwkv"]), bf(prm["p_mla"]), seq, tm_x, 512)
    h2 = _out_proj(merged, bf(prm["w_o"]), hp, seq, tm_x, 512).reshape(b * seq, d)

    keys = bf(prm["peer_keys"].reshape(PEER_HEADS * 2, N_KEYS, -1))
    g_ffn = prm["norm_ffn_g"][None]
    experts, gates = _route(h2, g_ffn, bf(prm["peer_wq"]), keys, _pick(b * seq, (256, 128)))
    return experts, gates, h2, g_ffn


def kernel(x, meta_tokens, norm_mix_g, w_in, b_gate, shift_c, w_up, w0, a_up, a0, g_up, k_k, k_a, r_k,
           ln_x_g, ln_x_b, q_norm_g, w_uq, kv_norm_g, w_ukv, p_rwkv, p_mla, w_o, norm_ffn_g,
           peer_wq, peer_keys, peer_u, peer_v, final_norm_g):
    b, seq, d = x.shape
    assert norm_mix_g.shape[0] == 1, "single-layer block"
    assert seq % 128 == 0 and meta_tokens.shape[0] == N_META
    meta = jnp.broadcast_to(meta_tokens.astype(x.dtype)[None], (b, N_META, d))
    hp = jnp.concatenate([x, jnp.zeros((b, PAD_ROWS, d), x.dtype), meta], axis=1)
    prm = dict(norm_mix_g=norm_mix_g[0], w_in=w_in[0], b_gate=b_gate[0], shift_c=shift_c[0], w_up=w_up[0],
               w0=w0[0], a_up=a_up[0], a0=a0[0], g_up=g_up[0], k_k=k_k[0], k_a=k_a[0], r_k=r_k[0],
               ln_x_g=ln_x_g[0], ln_x_b=ln_x_b[0], q_norm_g=q_norm_g[0], w_uq=w_uq[0], kv_norm_g=kv_norm_g[0],
               w_ukv=w_ukv[0], p_rwkv=p_rwkv[0], p_mla=p_mla[0], w_o=w_o[0], norm_ffn_g=norm_ffn_g[0],
               peer_wq=peer_wq[0], peer_keys=peer_keys[0])
    table = _pack_tables(peer_u[0], peer_v[0])
    experts, gates, h2, g_ffn = _layer(hp, seq, prm, table)
    out = _experts(experts, h2, gates, g_ffn, final_norm_g[None], table)
    return out.reshape(b, seq, d)
```

```python
import functools
import math

import jax
import jax.numpy as jnp
import numpy as np
from jax import lax
from jax.experimental import pallas as pl
from jax.experimental.pallas import tpu as pltpu

F32 = jnp.float32
BF16 = jnp.bfloat16

N_META = 16
PAD_ROWS = 112
TAIL = PAD_ROWS + N_META
NORM_EPS = 1e-6
GN_EPS = 64e-5
ROPE_THETA = 10000.0

RWKV_HEADS = 16
RWKV_HEAD = 64
RWKV_DIM = RWKV_HEADS * RWKV_HEAD
GATE_LORA = 160
LORA = 64
CHUNK = 64

MLA_HEADS = 16
QK_NOPE = 128
QK_ROPE = 64
V_HEAD = 128
QK_PAD = 256

N_KEYS = 128
PEER_HEADS = 8
PEER_TOPK = 16
PEER_SEL = PEER_HEADS * PEER_TOPK
PEER_TOK = 8

VMEM_LIMIT = 56 * 1024 * 1024


def _cp(sem, vmem=VMEM_LIMIT):
    return pltpu.CompilerParams(dimension_semantics=sem, vmem_limit_bytes=vmem)


def _pick(n, cands):
    for c in cands:
        if n % c == 0:
            return c
    raise ValueError(f"no tile for {n}")


def _dot(a, b):
    return jnp.dot(a, b, preferred_element_type=F32)


def _dot_nt(a, b):
    return lax.dot_general(a, b, (((1,), (1,)), ((), ())), preferred_element_type=F32)


def _split_bf16(x):
    hi = x.astype(BF16)
    lo = (x - hi.astype(F32)).astype(BF16)
    return hi, lo


def _dot2(x, w):
    hi, lo = _split_bf16(x)
    return _dot(hi, w) + _dot(lo, w)


def _seg_sum(x, red, spread):
    return _dot2(_dot2(x, red), spread)


def _norm_rows(x, g):
    ms = jnp.mean(x * x, axis=-1, keepdims=True)
    return x * lax.rsqrt(ms + NORM_EPS) * g


def _norm_mm_kernel(x_ref, g_ref, w_ref, o_ref, nb_ref):
    @pl.when(pl.program_id(2) == 0)
    def _():
        nb_ref[...] = _norm_rows(x_ref[...], g_ref[...]).astype(BF16)

    o_ref[...] = _dot(nb_ref[...], w_ref[...]).astype(o_ref.dtype)


def _norm_mm(x, g, w, tm, tn, out_dtype=F32, name="norm_mm"):
    b, l, k = x.shape
    n = w.shape[1]
    return pl.pallas_call(
        _norm_mm_kernel, grid=(b, l // tm, n // tn),
        in_specs=[pl.BlockSpec((None, tm, k), lambda bi, i, j: (bi, i, 0)),
                  pl.BlockSpec((1, k), lambda bi, i, j: (0, 0)),
                  pl.BlockSpec((k, tn), lambda bi, i, j: (0, j))],
        out_specs=pl.BlockSpec((None, tm, tn), lambda bi, i, j: (bi, i, j)),
        out_shape=jax.ShapeDtypeStruct((b, l, n), out_dtype),
        scratch_shapes=[pltpu.VMEM((tm, k), BF16)],
        compiler_params=_cp(("parallel", "parallel", "arbitrary")), name=name)(x, g, w)


def _rwkv_prep_kernel(p_ref, pp_ref, pn_ref, sc_ref, wl_ref, w0_ref, a0_ref, gup_ref, kkw_ref, ka_ref, rk_ref,
                      red_ref, spread_ref, r_ref, v_ref, kk_ref, lwf_ref, lwb_ref, ktf_ref, ktb_ref, bf_ref, bb_ref,
                      g_ref, bonus_ref, *, seq, tm):
    p = p_ref[...]
    row = lax.broadcasted_iota(jnp.int32, p.shape, 0)
    prev = jnp.where(row == 0, pp_ref[7:8, :], pltpu.roll(p, 1, 0))
    nxt = jnp.where(row == tm - 1, pn_ref[0:1, :], pltpu.roll(p, tm - 1, 0))
    xs = sc_ref[0:1, :] * prev + sc_ref[1:2, :] * p + sc_ref[2:3, :] * nxt
    c = RWKV_DIM
    r, k, v = xs[:, :c], xs[:, c:2 * c], xs[:, 2 * c:3 * c]
    lora = xs[:, 3 * c:3 * c + 4 * LORA]
    lane = lax.broadcasted_iota(jnp.int32, lora.shape, 1)
    lora = jnp.where(lane < 2 * LORA, jnp.tanh(lora), lora)
    lo = _dot(lora.astype(BF16), wl_ref[...])
    g = _dot(jax.nn.sigmoid(xs[:, 3 * c + 4 * LORA:]).astype(BF16), gup_ref[...])
    red, spread = red_ref[...], spread_ref[...]
    kk = k * kkw_ref[...]
    nrm = jnp.sqrt(_seg_sum(kk * kk, red, spread))
    kk = kk / jnp.maximum(nrm, 1e-12)
    grow = pl.program_id(1) * tm + lax.broadcasted_iota(jnp.int32, v.shape, 0)
    v = jnp.where((grow < seq) | (grow >= seq + PAD_ROWS), v, 0.0)
    ka = ka_ref[...]
    kts = []
    for d, (lw_ref, kt_ref, b_ref) in enumerate(((lwf_ref, ktf_ref, bf_ref), (lwb_ref, ktb_ref, bb_ref))):
        w_log = -jax.nn.softplus(-(w0_ref[d:d + 1, :] + lo[:, d * c:(d + 1) * c])) - 0.5
        lw_ref[...] = -jnp.exp(w_log)
        iclr = jax.nn.sigmoid(a0_ref[d:d + 1, :] + lo[:, (2 + d) * c:(3 + d) * c])
        kt = k * (1.0 + (iclr - 1.0) * ka)
        kt_ref[...] = kt
        b_ref[...] = kk * iclr
        kts.append(kt)
    r_ref[...] = r
    v_ref[...] = v
    kk_ref[...] = kk
    g_ref[...] = g
    bonus_ref[...] = _seg_sum(r * (0.5 * (kts[0] + kts[1])) * rk_ref[...], red, spread) * v


def _rwkv_prep(pr, shift_c, w_lora, w0, a0, g_up, k_k, k_a, r_k, red, spread, seq, tm):
    b, l, cols = pr.shape
    c = RWKV_DIM
    nb8 = l // 8
    row = lambda bi, i: (bi, i, 0)
    full = lambda a: pl.BlockSpec(a.shape, lambda bi, i: (0,) * a.ndim)
    out = jax.ShapeDtypeStruct((b, l, c), F32)
    ospec = pl.BlockSpec((None, tm, c), row)
    return pl.pallas_call(
        functools.partial(_rwkv_prep_kernel, seq=seq, tm=tm), grid=(b, l // tm),
        in_specs=[pl.BlockSpec((None, tm, cols), row),
                  pl.BlockSpec((None, 8, cols), lambda bi, i: (bi, (i * (tm // 8) + nb8 - 1) % nb8, 0)),
                  pl.BlockSpec((None, 8, cols), lambda bi, i: (bi, ((i + 1) * (tm // 8)) % nb8, 0)),
                  full(shift_c), full(w_lora), full(w0), full(a0), full(g_up), full(k_k), full(k_a), full(r_k),
                  full(red), full(spread)],
        out_specs=[ospec] * 11, out_shape=[out] * 11,
        compiler_params=_cp(("parallel", "parallel")), name="rwkv_prep",
    )(pr, pr, pr, shift_c, w_lora, w0, a0, g_up, k_k, k_a, r_k, red, spread)


def _scan_kernel(r_ref, lw_ref, kt_ref, v_ref, kk_ref, b_ref, tri_ref, y_ref, ht_ref, *, reverse):
    @pl.when(pl.program_id(1) == 0)
    def _():
        ht_ref[...] = jnp.zeros_like(ht_ref)

    lw = lw_ref[...]
    hi, lo = _split_bf16(lw)
    tri = tri_ref[...]
    cum = _dot(tri, hi) + _dot(tri, lo)
    tot = cum[0:1] if reverse else cum[CHUNK - 1:CHUNK]
    e_in = jnp.exp(cum)
    e_inv = jnp.exp(-cum)
    e_ex = jnp.exp(cum - lw)
    e_tot = jnp.exp(tot)
    at_all = -(kk_ref[...] * e_ex)
    rt_all = r_ref[...] * e_in
    bt_all = b_ref[...] * e_inv
    kt_all = kt_ref[...] * e_inv
    v_all = v_ref[...]

    n2 = 2 * CHUNK
    ri = lax.broadcasted_iota(jnp.int32, (n2, n2), 0)
    ci = lax.broadcasted_iota(jnp.int32, (n2, n2), 1)
    tt, ss = ri % CHUNK, ci % CHUNK
    strict = (tt < ss) if reverse else (tt > ss)
    incl = (tt <= ss) if reverse else (tt >= ss)
    bd = (ri < CHUNK) == (ci < CHUNK)
    eye = (ri == ci).astype(F32)
    head0 = lax.broadcasted_iota(jnp.int32, (CHUNK, n2), 1) < RWKV_HEAD
    cat = jnp.concatenate
    bf = lambda t: t.astype(BF16)

    pairs = range(RWKV_HEADS // 2)
    sls = [slice(p * n2, (p + 1) * n2) for p in pairs]
    h_old = [ht_ref[p] for p in pairs]
    at = [at_all[:, sl] for sl in sls]
    rt = [rt_all[:, sl] for sl in sls]
    bt = [bt_all[:, sl] for sl in sls]
    kt = [kt_all[:, sl] for sl in sls]
    v = [v_all[:, sl] for sl in sls]
    at0 = [jnp.where(head0, t, 0.0) for t in at]
    rt0 = [jnp.where(head0, t, 0.0) for t in rt]
    v0 = [jnp.where(head0, t, 0.0) for t in v]
    g0 = [_dot_nt(bf(cat([at0[p], rt0[p]], 0)), bf(cat([bt[p], kt[p]], 0))) for p in pairs]
    g1 = [_dot_nt(bf(cat([at[p] - at0[p], rt[p] - rt0[p]], 0)), bf(cat([kt[p], bt[p]], 0))) for p in pairs]
    ga = [jnp.where(strict, cat([g0[p][:CHUNK], g1[p][:CHUNK]], 0), 0.0) for p in pairs]
    gr = [jnp.where(incl, cat([g0[p][CHUNK:], g1[p][CHUNK:]], 0), 0.0) for p in pairs]
    pk = [jnp.where(bd, t, 0.0) for t in ga]
    ga_anti = [bf(ga[p] - pk[p]) for p in pairs]
    tinv = [eye + t for t in pk]
    for _ in range(int(math.log2(CHUNK)) - 1):
        pk = [_dot(bf(t), bf(t)) for t in pk]
        tinv = [tinv[p] + _dot(bf(tinv[p]), bf(pk[p])) for p in pairs]
    ht = [bf(t) for t in h_old]
    vx = [bf(cat([v[p] - v0[p], v0[p]], 0)) for p in pairs]
    ws = [_dot_nt(bf(cat([at0[p], at[p] - at0[p]], 0)), ht[p]) + _dot(ga_anti[p], vx[p]) for p in pairs]
    us = [_dot(bf(tinv[p]), bf(ws[p])) for p in pairs]
    gr_d = [jnp.where(bd, t, 0.0) for t in gr]
    ys = [_dot_nt(bf(cat([rt0[p], rt[p] - rt0[p]], 0)), ht[p])
          + _dot(bf(cat([gr_d[p], gr[p] - gr_d[p]], 1)), cat([bf(us[p]), vx[p]], 0)) for p in pairs]
    et = [e_tot[:, sl] for sl in sls]
    hn = [_dot(bf(cat([us[p][:CHUNK] + us[p][CHUNK:], v[p]], 0).T), bf(cat([bt[p] * et[p], kt[p] * et[p]], 0)))
          for p in pairs]
    for p in pairs:
        y_ref[:, sls[p]] = ys[p][:CHUNK] + ys[p][CHUNK:]
        ht_ref[p] = et[p] * h_old[p] + jnp.where(bd, hn[p], 0.0)


def _scan(r, lw, kt, v, kk, bb, tri, seq, reverse):
    b, l, c = r.shape
    nmem = l // CHUNK
    nx = seq // CHUNK
    nc = nmem
    if reverse:
        cmap = lambda bi, ci: (bi, jnp.where(ci < nx, nx - 1 - ci, nx + nmem - 1 - ci), 0)
    else:
        cmap = lambda bi, ci: (bi, (ci + nmem - 1) % nmem, 0)
    spec = pl.BlockSpec((None, CHUNK, c), cmap)
    return pl.pallas_call(
        functools.partial(_scan_kernel, reverse=reverse), grid=(b, nc),
        in_specs=[spec] * 6 + [pl.BlockSpec((CHUNK, CHUNK), lambda bi, ci: (0, 0))],
        out_specs=spec, out_shape=jax.ShapeDtypeStruct((b, l, c), F32),
        scratch_shapes=[pltpu.VMEM((RWKV_HEADS // 2, 2 * CHUNK, 2 * CHUNK), F32)],
        compiler_params=_cp(("parallel", "arbitrary")), name="wkv_bwd" if reverse else "wkv_fwd",
    )(r, lw, kt, v, kk, bb, tri)


def _q_proj_kernel(x_ref, g_ref, w_ref, cs_ref, sn_ref, o_ref, *, scale):
    acc = _dot(_norm_rows(x_ref[...], g_ref[...]).astype(BF16), w_ref[...])
    cs = cs_ref[...] * scale
    sn = sn_ref[...] * scale
    hw = MLA_HEADS * QK_NOPE
    for h in range(MLA_HEADS):
        lo = h * QK_NOPE
        o_ref[h, :, :QK_NOPE] = (acc[:, lo:lo + QK_NOPE] * scale).astype(o_ref.dtype)
        rot = acc[:, hw + lo:hw + lo + QK_NOPE] * cs + acc[:, 2 * hw + lo:2 * hw + lo + QK_NOPE] * sn
        o_ref[h, :, QK_NOPE:] = rot.astype(o_ref.dtype)


def _q_proj(pm, g, wq, cs, sn, seq, tm, scale):
    b = pm.shape[0]
    k = g.shape[1]
    return pl.pallas_call(
        functools.partial(_q_proj_kernel, scale=scale), grid=(b, seq // tm),
        in_specs=[pl.BlockSpec((None, tm, k), lambda bi, i: (bi, i, 0)),
                  pl.BlockSpec((1, k), lambda bi, i: (0, 0)),
                  pl.BlockSpec(wq.shape, lambda bi, i: (0, 0)),
                  pl.BlockSpec((tm, QK_NOPE), lambda bi, i: (i, 0)),
                  pl.BlockSpec((tm, QK_NOPE), lambda bi, i: (i, 0))],
        out_specs=pl.BlockSpec((None, MLA_HEADS, tm, QK_PAD), lambda bi, i: (bi, 0, i, 0)),
        out_shape=jax.ShapeDtypeStruct((b, MLA_HEADS, seq, QK_PAD), BF16),
        compiler_params=_cp(("parallel", "parallel")), name="mla_q")(pm, g, wq, cs, sn)


def _kv_proj_kernel(x_ref, g_ref, w_ref, ka_ref, kb_ref, cs_ref, sn_ref, k_ref, v_ref):
    acc = _dot(_norm_rows(x_ref[...], g_ref[...]).astype(BF16), w_ref[...])
    rope = (ka_ref[...] * cs_ref[...] + kb_ref[...] * sn_ref[...]).astype(k_ref.dtype)
    hw = MLA_HEADS * QK_NOPE
    for h in range(MLA_HEADS):
        lo = h * QK_NOPE
        k_ref[h, :, :QK_NOPE] = acc[:, lo:lo + QK_NOPE].astype(k_ref.dtype)
        k_ref[h, :, QK_NOPE:] = rope
        v_ref[h] = acc[:, hw + lo:hw + lo + V_HEAD].astype(v_ref.dtype)


def _kv_proj(pm, g, wkv, cs, sn, tm):
    b, l, _ = pm.shape
    k = g.shape[1]
    kb = k // QK_NOPE
    return pl.pallas_call(
        _kv_proj_kernel, grid=(b, l // tm),
        in_specs=[pl.BlockSpec((None, tm, k), lambda bi, i: (bi, i, 1)),
                  pl.BlockSpec((1, k), lambda bi, i: (0, 0)),
                  pl.BlockSpec(wkv.shape, lambda bi, i: (0, 0)),
                  pl.BlockSpec((None, tm, QK_NOPE), lambda bi, i: (bi, i, 2 * kb)),
                  pl.BlockSpec((None, tm, QK_NOPE), lambda bi, i: (bi, i, 2 * kb + 1)),
                  pl.BlockSpec((tm, QK_NOPE), lambda bi, i: (i, 0)),
                  pl.BlockSpec((tm, QK_NOPE), lambda bi, i: (i, 0))],
        out_specs=[pl.BlockSpec((None, MLA_HEADS, tm, QK_PAD), lambda bi, i: (bi, 0, i, 0)),
                   pl.BlockSpec((None, MLA_HEADS, tm, V_HEAD), lambda bi, i: (bi, 0, i, 0))],
        out_shape=[jax.ShapeDtypeStruct((b, MLA_HEADS, l, QK_PAD), BF16),
                   jax.ShapeDtypeStruct((b, MLA_HEADS, l, V_HEAD), BF16)],
        compiler_params=_cp(("parallel", "parallel")), name="mla_kv")(pm, g, wkv, pm, pm, cs, sn)


def _attn_kernel(q_ref, k_ref, v_ref, bias_ref, o_ref, *, seq, bk):
    q = q_ref[...]
    bq = q.shape[0]

    def update(carry, s, v):
        m, den, acc = carry
        m_new = jnp.maximum(m, jnp.max(s, axis=-1, keepdims=True))
        alpha = jnp.exp2(m - m_new)
        p = jnp.exp2(s - m_new)
        return (m_new, alpha * den + jnp.sum(p, axis=-1, keepdims=True), alpha * acc + _dot(p.astype(BF16), v))

    s_t = _dot_nt(q, k_ref[seq:, :]) + bias_ref[...]
    carry = update((jnp.full((bq, 1), -jnp.inf, F32), jnp.zeros((bq, 1), F32), jnp.zeros((bq, V_HEAD), F32)),
                   s_t, v_ref[seq:, :])
    for c in range(seq // bk):
        carry = update(carry, _dot_nt(q, k_ref[c * bk:(c + 1) * bk, :]), v_ref[c * bk:(c + 1) * bk, :])
    _, den, acc = carry
    o_ref[...] = (acc / den).astype(o_ref.dtype)


def _attention(q, k, v, bias, seq, bq):
    b, h, l, _ = k.shape
    return pl.pallas_call(
        functools.partial(_attn_kernel, seq=seq, bk=_pick(seq, (1024, 512, 256, 128))), grid=(b, h, seq // bq),
        in_specs=[pl.BlockSpec((None, None, bq, QK_PAD), lambda bi, hi, i: (bi, hi, i, 0)),
                  pl.BlockSpec((None, None, l, QK_PAD), lambda bi, hi, i: (bi, hi, 0, 0)),
                  pl.BlockSpec((None, None, l, V_HEAD), lambda bi, hi, i: (bi, hi, 0, 0)),
                  pl.BlockSpec((1, TAIL), lambda bi, hi, i: (0, 0))],
        out_specs=pl.BlockSpec((None, bq, V_HEAD), lambda bi, hi, i: (bi, i, hi)),
        out_shape=jax.ShapeDtypeStruct((b, seq, h * V_HEAD), BF16),
        compiler_params=_cp(("parallel", "parallel", "arbitrary")), name="mla_attn")(q, k, v, bias)


def _merge_kernel(yf_ref, yb_ref, bonus_ref, g_ref, lng_ref, lnb_ref, red_ref, spread_ref, ym_ref, gr_ref, gm_ref,
                  bgr_ref, bgm_ref, pr_ref, pm_ref, o_ref, yr_ref):
    @pl.when(pl.program_id(2) == 0)
    def _():
        y = yf_ref[...] + yb_ref[...]
        red, spread = red_ref[...], spread_ref[...]
        d = y - _seg_sum(y, red, spread)
        var = _seg_sum(d * d, red, spread)
        yn = d * lax.rsqrt(var + GN_EPS) * lng_ref[...] + lnb_ref[...]
        yr_ref[...] = ((yn + bonus_ref[...]) * g_ref[...]).astype(BF16)

    a_r = _dot(yr_ref[...], pr_ref[...])
    a_m = _dot(ym_ref[...], pm_ref[...])
    merged = jax.nn.sigmoid(gr_ref[...] + bgr_ref[...]) * a_r + jax.nn.sigmoid(gm_ref[...] + bgm_ref[...]) * a_m
    o_ref[...] = merged.astype(o_ref.dtype)


def _merge(yf, yb, bonus, g, ln_g, ln_b, red, spread, ym, pg, b_gate, p_rwkv, p_mla, seq, tm, tn):
    b = yf.shape[0]
    c = RWKV_DIM
    d = p_rwkv.shape[1]
    nj = d // tn
    rowc = pl.BlockSpec((None, tm, c), lambda bi, i, j: (bi, i, 0))
    vec = pl.BlockSpec((1, c), lambda bi, i, j: (0, 0))
    return pl.pallas_call(
        _merge_kernel, grid=(b, seq // tm, nj),
        in_specs=[rowc, rowc, rowc, rowc, vec, vec,
                  pl.BlockSpec(red.shape, lambda bi, i, j: (0, 0)),
                  pl.BlockSpec(spread.shape, lambda bi, i, j: (0, 0)),
                  pl.BlockSpec((None, tm, ym.shape[2]), lambda bi, i, j: (bi, i, 0)),
                  pl.BlockSpec((None, tm, tn), lambda bi, i, j: (bi, i, j)),
                  pl.BlockSpec((None, tm, tn), lambda bi, i, j: (bi, i, j + nj)),
                  pl.BlockSpec((1, tn), lambda bi, i, j: (0, j)),
                  pl.BlockSpec((1, tn), lambda bi, i, j: (0, j + nj)),
                  pl.BlockSpec((c, tn), lambda bi, i, j: (0, j)),
                  pl.BlockSpec((p_mla.shape[0], tn), lambda bi, i, j: (0, j))],
        out_specs=pl.BlockSpec((None, tm, tn), lambda bi, i, j: (bi, i, j)),
        out_shape=jax.ShapeDtypeStruct((b, seq, d), BF16),
        scratch_shapes=[pltpu.VMEM((tm, c), BF16)],
        compiler_params=_cp(("parallel", "parallel", "arbitrary")), name="merge",
    )(yf, yb, bonus, g, ln_g, ln_b, red, spread, ym, pg, pg, b_gate, b_gate, p_rwkv, p_mla)


def _out_proj_kernel(a_ref, w_ref, h_ref, o_ref):
    o_ref[...] = h_ref[...] + _dot(a_ref[...], w_ref[...])


def _out_proj(merged, w_o, hp, seq, tm, tn):
    b, _, d = merged.shape
    return pl.pallas_call(
        _out_proj_kernel, grid=(b, seq // tm, d // tn),
        in_specs=[pl.BlockSpec((None, tm, d), lambda bi, i, j: (bi, i, 0)),
                  pl.BlockSpec((d, tn), lambda bi, i, j: (0, j)),
                  pl.BlockSpec((None, tm, tn), lambda bi, i, j: (bi, i, j))],
        out_specs=pl.BlockSpec((None, tm, tn), lambda bi, i, j: (bi, i, j)),
        out_shape=jax.ShapeDtypeStruct((b, seq, d), F32),
        compiler_params=_cp(("parallel", "parallel", "arbitrary")), name="out_proj")(merged, w_o, hp)


def _topk_rows(s, order, ids, k):
    vals, sel = [], []
    for _ in range(k):
        m = jnp.max(s, axis=0, keepdims=True)
        first = jnp.min(jnp.where(s == m, order, jnp.int32(2 ** 30)), axis=0, keepdims=True)
        hit = order == first
        vals.append(m)
        sel.append(first if ids is None else jnp.max(jnp.where(hit, ids, -1), axis=0, keepdims=True))
        s = jnp.where(hit, -jnp.inf, s)
    return jnp.concatenate(vals, 0), jnp.concatenate(sel, 0)


def _pair_candidates(s1, i1, s2, i2):
    k = PEER_TOPK
    assert k == 16
    tm = s1.shape[1]
    sub8 = lax.broadcasted_iota(jnp.int32, (8, tm), 0)
    sub16 = lax.broadcasted_iota(jnp.int32, (16, tm), 0)
    ninf = -jnp.inf
    vals, order, eid = [], [], []

    def add(val, o, e, valid=None):
        vals.append(val if valid is None else jnp.where(valid, val, ninf))
        order.append(o)
        eid.append(e)

    add(s1[0:1] + s2, sub16, i1[0:1] * N_KEYS + i2)
    for a, nb in ((1, 8), (2, 5), (3, 4)):
        add(s1[a:a + 1] + s2[0:8], a * k + sub8, i1[a:a + 1] * N_KEYS + i2[0:8], None if nb == 8 else sub8 < nb)
    for b, amax in ((0, 7), (1, 7), (2, 4)):
        add(s1[0:8] + s2[b:b + 1], sub8 * k + b, i1[0:8] * N_KEYS + i2[b:b + 1], (sub8 >= 4) & (sub8 <= amax))
    add(s1[8:16] + s2[0:1], (sub8 + 8) * k, i1[8:16] * N_KEYS + i2[0:1])
    return jnp.concatenate(vals, 0), jnp.concatenate(order, 0), jnp.concatenate(eid, 0)


def _route_kernel(h_ref, g_ref, wq_ref, keys_ref, e_ref, gate_ref):
    nb = _norm_rows(h_ref[...], g_ref[...]).astype(BF16)
    q = _dot(nb, wq_ref[...]).astype(BF16)
    tm = q.shape[0]
    key_rows = lax.broadcasted_iota(jnp.int32, (N_KEYS, tm), 0)
    e_rows, g_rows = [], []
    for h in range(PEER_HEADS):
        tops = []
        for half in range(2):
            gidx = 2 * h + half
            s = _dot_nt(keys_ref[gidx], q[:, gidx * N_KEYS:(gidx + 1) * N_KEYS])
            tops.append(_topk_rows(s, key_rows, None, PEER_TOPK))
        (s1, i1), (s2, i2) = tops
        best, experts = _topk_rows(*_pair_candidates(s1, i1, s2, i2), PEER_TOPK)
        ex = jnp.exp(best - best[0:1])
        g_rows.append(ex / jnp.sum(ex, axis=0, keepdims=True))
        e_rows.append(experts)
    e_ref[...] = jnp.concatenate(e_rows, 0).T
    gate_ref[...] = jnp.concatenate(g_rows, 0).T


def _route(h2, g, wq, keys, tm):
    t, d = h2.shape
    return pl.pallas_call(
        _route_kernel, grid=(t // tm,),
        in_specs=[pl.BlockSpec((tm, d), lambda i: (i, 0)),
                  pl.BlockSpec((1, d), lambda i: (0, 0)),
                  pl.BlockSpec(wq.shape, lambda i: (0, 0)),
                  pl.BlockSpec(keys.shape, lambda i: (0, 0, 0))],
        out_specs=[pl.BlockSpec((tm, PEER_SEL), lambda i: (i, 0))] * 2,
        out_shape=[jax.ShapeDtypeStruct((t, PEER_SEL), jnp.int32), jax.ShapeDtypeStruct((t, PEER_SEL), F32)],
        compiler_params=_cp(("parallel",)), name="peer_route")(h2, g, wq, keys)


def _expert_kernel(idx_ref, idxn_ref, h_ref, gate_ref, gffn_ref, gfin_ref, tab_ref, o_ref, buf_ref, sem_ref):
    i = pl.program_id(0)
    n = pl.num_programs(0)
    rows = PEER_TOK * PEER_SEL

    def row_copy(src_ref, r, dst_slot):
        return pltpu.make_async_copy(tab_ref.at[src_ref[0, 0, r]], buf_ref.at[dst_slot, pl.ds(r, 1)],
                                     sem_ref.at[dst_slot])

    @pl.when(i == 0)
    def _():
        def body(r, carry):
            row_copy(idx_ref, r, 0).start()
            return carry
        lax.fori_loop(0, rows, body, 0, unroll=8)

    def wait_slot(s):
        pltpu.make_async_copy(buf_ref.at[s], buf_ref.at[s], sem_ref.at[s]).wait()

    def step(slot):
        groups = 2 * PEER_TOK
        per = rows // groups

        def issue(gi):
            for r in range(gi * per, (gi + 1) * per):
                row_copy(idxn_ref, r, 1 - slot).start(priority=r % 2)

        h = h_ref[...]
        d = h.shape[1]
        nb = _norm_rows(h, gffn_ref[...]).astype(BF16)
        gate = gate_ref[...]
        trow = lax.broadcasted_iota(jnp.int32, gate.shape, 0)
        wait_slot(slot)
        s_blocks = []
        for t in range(PEER_TOK):
            u = buf_ref[slot, pl.ds(t * PEER_SEL, PEER_SEL), pl.ds(0, d)]
            s_blocks.append(_dot_nt(nb, u.astype(BF16)))
            issue(t)
        s = jnp.concatenate(s_blocks, axis=1)
        gmat = jnp.concatenate([jnp.where(trow == t, gate, 0.0) for t in range(PEER_TOK)], axis=1)
        act = (0.5 * s * (1.0 + lax.erf(s * (2.0 ** -0.5))) * gmat).astype(BF16)
        y = h
        for t in range(PEER_TOK):
            v = buf_ref[slot, pl.ds(t * PEER_SEL, PEER_SEL), pl.ds(d, d)]
            y = y + _dot(act[:, t * PEER_SEL:(t + 1) * PEER_SEL], v.astype(BF16))
            issue(PEER_TOK + t)
        o_ref[...] = _norm_rows(y, gfin_ref[...])

        @pl.when(i == n - 1)
        def _():
            wait_slot(1 - slot)

    for parity in range(2):
        pl.when(i % 2 == parity)(functools.partial(step, parity))


def _experts(idx, h2, gate, g_ffn, g_fin, table):
    t, d = h2.shape
    nt = t // PEER_TOK
    rows = PEER_TOK * PEER_SEL
    idx3 = idx.reshape(nt, 1, rows)
    return pl.pallas_call(
        _expert_kernel, grid=(nt,),
        in_specs=[pl.BlockSpec((1, 1, rows), lambda i: (i, 0, 0), memory_space=pltpu.SMEM),
                  pl.BlockSpec((1, 1, rows), lambda i: (jnp.minimum(i + 1, nt - 1), 0, 0), memory_space=pltpu.SMEM),
                  pl.BlockSpec((PEER_TOK, d), lambda i: (i, 0)),
                  pl.BlockSpec((PEER_TOK, PEER_SEL), lambda i: (i, 0)),
                  pl.BlockSpec((1, d), lambda i: (0, 0)),
                  pl.BlockSpec((1, d), lambda i: (0, 0)),
                  pl.BlockSpec(memory_space=pl.ANY)],
        out_specs=pl.BlockSpec((PEER_TOK, d), lambda i: (i, 0)),
        out_shape=jax.ShapeDtypeStruct((t, d), F32),
        scratch_shapes=[pltpu.VMEM((2, rows, 2 * d), F32), pltpu.SemaphoreType.DMA((2,))],
        compiler_params=_cp(("arbitrary",)), name="peer_experts")(idx3, idx3, h2, gate, g_ffn, g_fin, table)


def _segment_mats(n, blk, val):
    hit = (jnp.arange(n) // blk)[:, None] == jnp.arange(128)[None, :]
    return jnp.where(hit, val, 0.0).astype(BF16), hit.T.astype(BF16)


def _layer(hp, seq, prm, table):
    b, l, d = hp.shape
    c = RWKV_DIM
    bf = lambda a: a.astype(BF16)
    w_in = prm["w_in"]
    o_q = c * 3 + GATE_LORA + 4 * LORA
    q_lora = prm["q_norm_g"].shape[0]
    kv_lora = prm["kv_norm_g"].shape[0]
    o_kr = o_q + q_lora + kv_lora
    o_g = o_kr + QK_ROPE

    def rw_cols(a):
        lead = a.shape[:-1]
        return jnp.concatenate([a[..., :3 * c], a[..., 3 * c + GATE_LORA:o_q], a[..., 3 * c:3 * c + GATE_LORA],
                                jnp.zeros(lead + (2 * QK_NOPE - GATE_LORA,), a.dtype)], -1)

    half = QK_ROPE // 2
    swap = lambda a: jnp.concatenate([a[..., half:], a[..., :half]], -1)
    padk = lambda a: jnp.concatenate([a, jnp.zeros(a.shape[:-1] + (QK_NOPE - a.shape[-1],), a.dtype)], -1)
    w_kr = w_in[:, o_kr:o_g]
    w_rw = bf(rw_cols(w_in))
    w_mla = bf(jnp.concatenate([w_in[:, o_q:o_kr], padk(w_kr), padk(swap(w_kr))], -1))
    w_gate = bf(w_in[:, o_g:])
    g_mix = prm["norm_mix_g"][None]

    tm_all = _pick(l, (384, 128))
    tm_in = _pick(l, (1408, 384, 128))
    pr = _norm_mm(hp, g_mix, w_rw, tm_in, _pick(w_rw.shape[1], (512, 256, 128)), name="in_rwkv")
    pm = _norm_mm(hp, g_mix, w_mla, tm_in, _pick(w_mla.shape[1], (640, 128)), name="in_mla")
    pg = _norm_mm(hp, g_mix, w_gate, tm_in, 512, name="in_gate")

    zl = jnp.zeros((LORA, c), F32)
    w_lora = bf(jnp.concatenate([
        jnp.concatenate([prm["w_up"][0], zl, zl, zl], 1), jnp.concatenate([zl, prm["w_up"][1], zl, zl], 1),
        jnp.concatenate([zl, zl, prm["a_up"][0], zl], 1), jnp.concatenate([zl, zl, zl, prm["a_up"][1]], 1)], 0))
    g_up = bf(jnp.concatenate([prm["g_up"], jnp.zeros((2 * QK_NOPE - GATE_LORA, c), F32)], 0))
    red_one, spread = _segment_mats(c, RWKV_HEAD, 1.0)
    r, v, kk, lwf, lwb, ktf, ktb, bfw, bbw, g, bonus = _rwkv_prep(
        pr, rw_cols(prm["shift_c"]), w_lora, prm["w0"], prm["a0"], g_up, prm["k_k"][None], prm["k_a"][None],
        prm["r_k"].reshape(1, c), red_one, spread, seq, 128)
    t_i = np.arange(CHUNK)
    tri_f = jnp.asarray(t_i[:, None] >= t_i[None, :], BF16)
    tri_b = jnp.asarray(t_i[:, None] <= t_i[None, :], BF16)
    y_f = _scan(r, lwf, ktf, v, kk, bfw, tri_f, seq, False)
    y_b = _scan(r, lwb, ktb, v, kk, bbw, tri_b, seq, True)

    mem = np.arange(l)
    pos = np.where(mem < seq, mem + N_META, np.maximum(mem - seq - PAD_ROWS, 0)).astype(np.float32)
    inv_freq = ROPE_THETA ** (-jnp.arange(0, QK_ROPE, 2, dtype=F32) / QK_ROPE)
    ang = jnp.asarray(pos)[:, None] * inv_freq[None, :]
    cos, sin = jnp.cos(ang), jnp.sin(ang)
    z2 = jnp.zeros((l, QK_NOPE - QK_ROPE), F32)
    cs = jnp.concatenate([cos, cos, z2], 1)
    sn = jnp.concatenate([-sin, sin, z2], 1)
    qh = prm["w_uq"].reshape(q_lora, MLA_HEADS, QK_NOPE + QK_ROPE)
    heads = lambda a: a.reshape(a.shape[0], -1)
    wq = bf(jnp.concatenate([heads(qh[..., :QK_NOPE]), heads(padk(qh[..., QK_NOPE:])),
                             heads(padk(swap(qh[..., QK_NOPE:])))], -1))
    kvh = prm["w_ukv"].reshape(kv_lora, MLA_HEADS, QK_NOPE + V_HEAD)
    wkv = bf(jnp.concatenate([heads(kvh[..., :QK_NOPE]), heads(kvh[..., QK_NOPE:])], -1))
    tm_x = _pick(seq, (512, 256, 128))
    scale = (QK_NOPE + QK_ROPE) ** -0.5 * math.log2(math.e)
    q = _q_proj(pm, prm["q_norm_g"][None], wq, cs, sn, seq, tm_x, scale)
    k, vv = _kv_proj(pm, prm["kv_norm_g"][None], wkv, cs, sn, tm_all)
    bias = jnp.asarray(np.where(np.arange(TAIL) < PAD_ROWS, -1e30, 0.0)[None], F32)
    y_m = _attention(q, k, vv, bias, seq, _pick(seq, (1024, 512, 256, 128)))

    red_avg, _ = _segment_mats(c, RWKV_HEAD, 1.0 / RWKV_HEAD)
    merged = _merge(y_f, y_b, bonus, g, prm["ln_x_g"][None], prm["ln_x_b"][None], red_avg, spread, y_m, pg,
                    prm["b_gate"][None], bf(prm["p_rwkv"]), bf(prm["p_mla"]), seq, tm_x, 512)
    h2 = _out_proj(merged, bf(prm["w_o"]), hp, seq, tm_x, 512).reshape(b * seq, d)

    keys = bf(prm["peer_keys"].reshape(PEER_HEADS * 2, N_KEYS, -1))
    g_ffn = prm["norm_ffn_g"][None]
    experts, gates = _route(h2, g_ffn, bf(prm["peer_wq"]), keys, _pick(b * seq, (256, 128)))
    return experts, gates, h2, g_ffn


def kernel(x, meta_tokens, norm_mix_g, w_in, b_gate, shift_c, w_up, w0, a_up, a0, g_up, k_k, k_a, r_k,
           ln_x_g, ln_x_b, q_norm_g, w_uq, kv_norm_g, w_ukv, p_rwkv, p_mla, w_o, norm_ffn_g,
           peer_wq, peer_keys, peer_u, peer_v, final_norm_g):
    b, seq, d = x.shape
    assert norm_mix_g.shape[0] == 1, "single-layer block"
    assert seq % 128 == 0 and meta_tokens.shape[0] == N_META
    meta = jnp.broadcast_to(meta_tokens.astype(x.dtype)[None], (b, N_META, d))
    hp = jnp.concatenate([x, jnp.zeros((b, PAD_ROWS, d), x.dtype), meta], axis=1)
    prm = dict(norm_mix_g=norm_mix_g[0], w_in=w_in[0], b_gate=b_gate[0], shift_c=shift_c[0], w_up=w_up[0],
               w0=w0[0], a_up=a_up[0], a0=a0[0], g_up=g_up[0], k_k=k_k[0], k_a=k_a[0], r_k=r_k[0],
               ln_x_g=ln_x_g[0], ln_x_b=ln_x_b[0], q_norm_g=q_norm_g[0], w_uq=w_uq[0], kv_norm_g=kv_norm_g[0],
               w_ukv=w_ukv[0], p_rwkv=p_rwkv[0], p_mla=p_mla[0], w_o=w_o[0], norm_ffn_g=norm_ffn_g[0],
               peer_wq=peer_wq[0], peer_keys=peer_keys[0])
    table = jnp.concatenate([peer_u[0], peer_v[0]], axis=1)[:, None, :]
    experts, gates, h2, g_ffn = _layer(hp, seq, prm, table)
    out = _experts(experts, h2, gates, g_ffn, final_norm_g[None], table)
    return out.reshape(b, seq, d)
```

```python
import functools
import math

import jax
import jax.numpy as jnp
import numpy as np
from jax import lax
from jax.experimental import pallas as pl
from jax.experimental.pallas import tpu as pltpu

F32 = jnp.float32
BF16 = jnp.bfloat16

N_META = 16
PAD_ROWS = 112
TAIL = PAD_ROWS + N_META
NORM_EPS = 1e-6
GN_EPS = 64e-5
ROPE_THETA = 10000.0

RWKV_HEADS = 16
RWKV_HEAD = 64
RWKV_DIM = RWKV_HEADS * RWKV_HEAD
GATE_LORA = 160
LORA = 64
CHUNK = 64

MLA_HEADS = 16
QK_NOPE = 128
QK_ROPE = 64
V_HEAD = 128
QK_PAD = 256

N_KEYS = 128
PEER_HEADS = 8
PEER_TOPK = 16
PEER_SEL = PEER_HEADS * PEER_TOPK
PEER_TOK = 8

VMEM_LIMIT = 56 * 1024 * 1024


def _cp(sem, vmem=VMEM_LIMIT):
    return pltpu.CompilerParams(dimension_semantics=sem, vmem_limit_bytes=vmem)


def _pick(n, cands):
    for c in cands:
        if n % c == 0:
            return c
    raise ValueError(f"no tile for {n}")


def _dot(a, b):
    return jnp.dot(a, b, preferred_element_type=F32)


def _dot_nt(a, b):
    return lax.dot_general(a, b, (((1,), (1,)), ((), ())), preferred_element_type=F32)


def _split_bf16(x):
    hi = x.astype(BF16)
    lo = (x - hi.astype(F32)).astype(BF16)
    return hi, lo


def _dot2(x, w):
    hi, lo = _split_bf16(x)
    return _dot(hi, w) + _dot(lo, w)


def _table_kernel(u_ref, v_ref, o_ref):
    o_ref[...] = jnp.concatenate([u_ref[...], v_ref[...]], axis=1)[:, None, :]


def _expert_table(u, v):
    n, d = u.shape
    tr = _pick(n, (256, 128, 8))
    spec = pl.BlockSpec((tr, d), lambda i: (i, 0))
    return pl.pallas_call(
        _table_kernel, grid=(n // tr,), in_specs=[spec, spec],
        out_specs=pl.BlockSpec((tr, 1, 2 * d), lambda i: (i, 0, 0)),
        out_shape=jax.ShapeDtypeStruct((n, 1, 2 * d), u.dtype),
        compiler_params=_cp(("parallel",)), name="peer_table")(u, v)


def _seg_sum(x, red, spread):
    return _dot2(_dot2(x, red), spread)


def _norm_rows(x, g):
    ms = jnp.mean(x * x, axis=-1, keepdims=True)
    return x * lax.rsqrt(ms + NORM_EPS) * g


def _norm_mm_kernel(x_ref, g_ref, w_ref, o_ref, nb_ref):
    @pl.when(pl.program_id(2) == 0)
    def _():
        nb_ref[...] = _norm_rows(x_ref[...], g_ref[...]).astype(BF16)

    o_ref[...] = _dot(nb_ref[...], w_ref[...]).astype(o_ref.dtype)


def _norm_mm(x, g, w, tm, tn, out_dtype=F32, name="norm_mm"):
    b, l, k = x.shape
    n = w.shape[1]
    return pl.pallas_call(
        _norm_mm_kernel, grid=(b, l // tm, n // tn),
        in_specs=[pl.BlockSpec((None, tm, k), lambda bi, i, j: (bi, i, 0)),
                  pl.BlockSpec((1, k), lambda bi, i, j: (0, 0)),
                  pl.BlockSpec((k, tn), lambda bi, i, j: (0, j))],
        out_specs=pl.BlockSpec((None, tm, tn), lambda bi, i, j: (bi, i, j)),
        out_shape=jax.ShapeDtypeStruct((b, l, n), out_dtype),
        scratch_shapes=[pltpu.VMEM((tm, k), BF16)],
        compiler_params=_cp(("parallel", "parallel", "arbitrary")), name=name)(x, g, w)


def _rwkv_prep_kernel(p_ref, pp_ref, pn_ref, sc_ref, wl_ref, w0_ref, a0_ref, gup_ref, kkw_ref, ka_ref, rk_ref,
                      red_ref, spread_ref, r_ref, v_ref, kk_ref, lwf_ref, lwb_ref, ktf_ref, ktb_ref, bf_ref, bb_ref,
                      g_ref, bonus_ref, *, seq, tm):
    p = p_ref[...]
    row = lax.broadcasted_iota(jnp.int32, p.shape, 0)
    prev = jnp.where(row == 0, pp_ref[7:8, :], pltpu.roll(p, 1, 0))
    nxt = jnp.where(row == tm - 1, pn_ref[0:1, :], pltpu.roll(p, tm - 1, 0))
    xs = sc_ref[0:1, :] * prev + sc_ref[1:2, :] * p + sc_ref[2:3, :] * nxt
    c = RWKV_DIM
    r, k, v = xs[:, :c], xs[:, c:2 * c], xs[:, 2 * c:3 * c]
    lora = xs[:, 3 * c:3 * c + 4 * LORA]
    lane = lax.broadcasted_iota(jnp.int32, lora.shape, 1)
    lora = jnp.where(lane < 2 * LORA, jnp.tanh(lora), lora)
    lo = _dot(lora.astype(BF16), wl_ref[...])
    g = _dot(jax.nn.sigmoid(xs[:, 3 * c + 4 * LORA:]).astype(BF16), gup_ref[...])
    red, spread = red_ref[...], spread_ref[...]
    kk = k * kkw_ref[...]
    nrm = jnp.sqrt(_seg_sum(kk * kk, red, spread))
    kk = kk / jnp.maximum(nrm, 1e-12)
    grow = pl.program_id(1) * tm + lax.broadcasted_iota(jnp.int32, v.shape, 0)
    v = jnp.where((grow < seq) | (grow >= seq + PAD_ROWS), v, 0.0)
    ka = ka_ref[...]
    kts = []
    for d, (lw_ref, kt_ref, b_ref) in enumerate(((lwf_ref, ktf_ref, bf_ref), (lwb_ref, ktb_ref, bb_ref))):
        w_log = -jax.nn.softplus(-(w0_ref[d:d + 1, :] + lo[:, d * c:(d + 1) * c])) - 0.5
        lw_ref[...] = -jnp.exp(w_log)
        iclr = jax.nn.sigmoid(a0_ref[d:d + 1, :] + lo[:, (2 + d) * c:(3 + d) * c])
        kt = k * (1.0 + (iclr - 1.0) * ka)
        kt_ref[...] = kt
        b_ref[...] = kk * iclr
        kts.append(kt)
    r_ref[...] = r
    v_ref[...] = v
    kk_ref[...] = kk
    g_ref[...] = g
    bonus_ref[...] = _seg_sum(r * (0.5 * (kts[0] + kts[1])) * rk_ref[...], red, spread) * v


def _rwkv_prep(pr, shift_c, w_lora, w0, a0, g_up, k_k, k_a, r_k, red, spread, seq, tm):
    b, l, cols = pr.shape
    c = RWKV_DIM
    nb8 = l // 8
    row = lambda bi, i: (bi, i, 0)
    full = lambda a: pl.BlockSpec(a.shape, lambda bi, i: (0,) * a.ndim)
    out = jax.ShapeDtypeStruct((b, l, c), F32)
    ospec = pl.BlockSpec((None, tm, c), row)
    return pl.pallas_call(
        functools.partial(_rwkv_prep_kernel, seq=seq, tm=tm), grid=(b, l // tm),
        in_specs=[pl.BlockSpec((None, tm, cols), row),
                  pl.BlockSpec((None, 8, cols), lambda bi, i: (bi, (i * (tm // 8) + nb8 - 1) % nb8, 0)),
                  pl.BlockSpec((None, 8, cols), lambda bi, i: (bi, ((i + 1) * (tm // 8)) % nb8, 0)),
                  full(shift_c), full(w_lora), full(w0), full(a0), full(g_up), full(k_k), full(k_a), full(r_k),
                  full(red), full(spread)],
        out_specs=[ospec] * 11, out_shape=[out] * 11,
        compiler_params=_cp(("parallel", "parallel")), name="rwkv_prep",
    )(pr, pr, pr, shift_c, w_lora, w0, a0, g_up, k_k, k_a, r_k, red, spread)


def _scan_kernel(r_ref, lw_ref, kt_ref, v_ref, kk_ref, b_ref, tri_ref, y_ref, ht_ref, *, reverse):
    @pl.when(pl.program_id(1) == 0)
    def _():
        ht_ref[...] = jnp.zeros_like(ht_ref)

    lw = lw_ref[...]
    hi, lo = _split_bf16(lw)
    tri = tri_ref[...]
    cum = _dot(tri, hi) + _dot(tri, lo)
    tot = cum[0:1] if reverse else cum[CHUNK - 1:CHUNK]
    e_in = jnp.exp(cum)
    e_inv = jnp.exp(-cum)
    e_ex = jnp.exp(cum - lw)
    e_tot = jnp.exp(tot)
    at_all = -(kk_ref[...] * e_ex)
    rt_all = r_ref[...] * e_in
    bt_all = b_ref[...] * e_inv
    kt_all = kt_ref[...] * e_inv
    v_all = v_ref[...]

    n2 = 2 * CHUNK
    ri = lax.broadcasted_iota(jnp.int32, (n2, n2), 0)
    ci = lax.broadcasted_iota(jnp.int32, (n2, n2), 1)
    tt, ss = ri % CHUNK, ci % CHUNK
    strict = (tt < ss) if reverse else (tt > ss)
    incl = (tt <= ss) if reverse else (tt >= ss)
    bd = (ri < CHUNK) == (ci < CHUNK)
    eye = (ri == ci).astype(F32)
    head0 = lax.broadcasted_iota(jnp.int32, (CHUNK, n2), 1) < RWKV_HEAD
    cat = jnp.concatenate
    bf = lambda t: t.astype(BF16)

    pairs = range(RWKV_HEADS // 2)
    sls = [slice(p * n2, (p + 1) * n2) for p in pairs]
    h_old = [ht_ref[p] for p in pairs]
    at = [at_all[:, sl] for sl in sls]
    rt = [rt_all[:, sl] for sl in sls]
    bt = [bt_all[:, sl] for sl in sls]
    kt = [kt_all[:, sl] for sl in sls]
    v = [v_all[:, sl] for sl in sls]
    at0 = [jnp.where(head0, t, 0.0) for t in at]
    rt0 = [jnp.where(head0, t, 0.0) for t in rt]
    v0 = [jnp.where(head0, t, 0.0) for t in v]
    g0 = [_dot_nt(bf(cat([at0[p], rt0[p]], 0)), bf(cat([bt[p], kt[p]], 0))) for p in pairs]
    g1 = [_dot_nt(bf(cat([at[p] - at0[p], rt[p] - rt0[p]], 0)), bf(cat([kt[p], bt[p]], 0))) for p in pairs]
    ga = [jnp.where(strict, cat([g0[p][:CHUNK], g1[p][:CHUNK]], 0), 0.0) for p in pairs]
    gr = [jnp.where(incl, cat([g0[p][CHUNK:], g1[p][CHUNK:]], 0), 0.0) for p in pairs]
    pk = [jnp.where(bd, t, 0.0) for t in ga]
    ga_anti = [bf(ga[p] - pk[p]) for p in pairs]
    tinv = [eye + t for t in pk]
    for _ in range(int(math.log2(CHUNK)) - 1):
        pk = [_dot(bf(t), bf(t)) for t in pk]
        tinv = [tinv[p] + _dot(bf(tinv[p]), bf(pk[p])) for p in pairs]
    ht = [bf(t) for t in h_old]
    vx = [bf(cat([v[p] - v0[p], v0[p]], 0)) for p in pairs]
    ws = [_dot_nt(bf(cat([at0[p], at[p] - at0[p]], 0)), ht[p]) + _dot(ga_anti[p], vx[p]) for p in pairs]
    us = [_dot(bf(tinv[p]), bf(ws[p])) for p in pairs]
    gr_d = [jnp.where(bd, t, 0.0) for t in gr]
    ys = [_dot_nt(bf(cat([rt0[p], rt[p] - rt0[p]], 0)), ht[p])
          + _dot(bf(cat([gr_d[p], gr[p] - gr_d[p]], 1)), cat([bf(us[p]), vx[p]], 0)) for p in pairs]
    et = [e_tot[:, sl] for sl in sls]
    hn = [_dot(bf(cat([us[p][:CHUNK] + us[p][CHUNK:], v[p]], 0).T), bf(cat([bt[p] * et[p], kt[p] * et[p]], 0)))
          for p in pairs]
    for p in pairs:
        y_ref[:, sls[p]] = ys[p][:CHUNK] + ys[p][CHUNK:]
        ht_ref[p] = et[p] * h_old[p] + jnp.where(bd, hn[p], 0.0)


def _scan(r, lw, kt, v, kk, bb, tri, seq, reverse):
    b, l, c = r.shape
    nmem = l // CHUNK
    nx = seq // CHUNK
    nc = nmem
    if reverse:
        cmap = lambda bi, ci: (bi, jnp.where(ci < nx, nx - 1 - ci, nx + nmem - 1 - ci), 0)
    else:
        cmap = lambda bi, ci: (bi, (ci + nmem - 1) % nmem, 0)
    spec = pl.BlockSpec((None, CHUNK, c), cmap)
    return pl.pallas_call(
        functools.partial(_scan_kernel, reverse=reverse), grid=(b, nc),
        in_specs=[spec] * 6 + [pl.BlockSpec((CHUNK, CHUNK), lambda bi, ci: (0, 0))],
        out_specs=spec, out_shape=jax.ShapeDtypeStruct((b, l, c), F32),
        scratch_shapes=[pltpu.VMEM((RWKV_HEADS // 2, 2 * CHUNK, 2 * CHUNK), F32)],
        compiler_params=_cp(("parallel", "arbitrary")), name="wkv_bwd" if reverse else "wkv_fwd",
    )(r, lw, kt, v, kk, bb, tri)


def _q_proj_kernel(x_ref, g_ref, w_ref, cs_ref, sn_ref, o_ref, *, scale):
    acc = _dot(_norm_rows(x_ref[...], g_ref[...]).astype(BF16), w_ref[...])
    cs = cs_ref[...] * scale
    sn = sn_ref[...] * scale
    hw = MLA_HEADS * QK_NOPE
    for h in range(MLA_HEADS):
        lo = h * QK_NOPE
        o_ref[h, :, :QK_NOPE] = (acc[:, lo:lo + QK_NOPE] * scale).astype(o_ref.dtype)
        rot = acc[:, hw + lo:hw + lo + QK_NOPE] * cs + acc[:, 2 * hw + lo:2 * hw + lo + QK_NOPE] * sn
        o_ref[h, :, QK_NOPE:] = rot.astype(o_ref.dtype)


def _q_proj(pm, g, wq, cs, sn, seq, tm, scale):
    b = pm.shape[0]
    k = g.shape[1]
    return pl.pallas_call(
        functools.partial(_q_proj_kernel, scale=scale), grid=(b, seq // tm),
        in_specs=[pl.BlockSpec((None, tm, k), lambda bi, i: (bi, i, 0)),
                  pl.BlockSpec((1, k), lambda bi, i: (0, 0)),
                  pl.BlockSpec(wq.shape, lambda bi, i: (0, 0)),
                  pl.BlockSpec((tm, QK_NOPE), lambda bi, i: (i, 0)),
                  pl.BlockSpec((tm, QK_NOPE), lambda bi, i: (i, 0))],
        out_specs=pl.BlockSpec((None, MLA_HEADS, tm, QK_PAD), lambda bi, i: (bi, 0, i, 0)),
        out_shape=jax.ShapeDtypeStruct((b, MLA_HEADS, seq, QK_PAD), BF16),
        compiler_params=_cp(("parallel", "parallel")), name="mla_q")(pm, g, wq, cs, sn)


def _kv_proj_kernel(x_ref, g_ref, w_ref, ka_ref, kb_ref, cs_ref, sn_ref, k_ref, v_ref):
    acc = _dot(_norm_rows(x_ref[...], g_ref[...]).astype(BF16), w_ref[...])
    rope = (ka_ref[...] * cs_ref[...] + kb_ref[...] * sn_ref[...]).astype(k_ref.dtype)
    hw = MLA_HEADS * QK_NOPE
    for h in range(MLA_HEADS):
        lo = h * QK_NOPE
        k_ref[h, :, :QK_NOPE] = acc[:, lo:lo + QK_NOPE].astype(k_ref.dtype)
        k_ref[h, :, QK_NOPE:] = rope
        v_ref[h] = acc[:, hw + lo:hw + lo + V_HEAD].astype(v_ref.dtype)


def _kv_proj(pm, g, wkv, cs, sn, tm):
    b, l, _ = pm.shape
    k = g.shape[1]
    kb = k // QK_NOPE
    return pl.pallas_call(
        _kv_proj_kernel, grid=(b, l // tm),
        in_specs=[pl.BlockSpec((None, tm, k), lambda bi, i: (bi, i, 1)),
                  pl.BlockSpec((1, k), lambda bi, i: (0, 0)),
                  pl.BlockSpec(wkv.shape, lambda bi, i: (0, 0)),
                  pl.BlockSpec((None, tm, QK_NOPE), lambda bi, i: (bi, i, 2 * kb)),
                  pl.BlockSpec((None, tm, QK_NOPE), lambda bi, i: (bi, i, 2 * kb + 1)),
                  pl.BlockSpec((tm, QK_NOPE), lambda bi, i: (i, 0)),
                  pl.BlockSpec((tm, QK_NOPE), lambda bi, i: (i, 0))],
        out_specs=[pl.BlockSpec((None, MLA_HEADS, tm, QK_PAD), lambda bi, i: (bi, 0, i, 0)),
                   pl.BlockSpec((None, MLA_HEADS, tm, V_HEAD), lambda bi, i: (bi, 0, i, 0))],
        out_shape=[jax.ShapeDtypeStruct((b, MLA_HEADS, l, QK_PAD), BF16),
                   jax.ShapeDtypeStruct((b, MLA_HEADS, l, V_HEAD), BF16)],
        compiler_params=_cp(("parallel", "parallel")), name="mla_kv")(pm, g, wkv, pm, pm, cs, sn)


def _attn_kernel(q_ref, k_ref, v_ref, bias_ref, o_ref, *, seq, bk):
    q = q_ref[...]
    bq = q.shape[0]

    def update(carry, s, v):
        m, den, acc = carry
        m_new = jnp.maximum(m, jnp.max(s, axis=-1, keepdims=True))
        alpha = jnp.exp2(m - m_new)
        p = jnp.exp2(s - m_new)
        return (m_new, alpha * den + jnp.sum(p, axis=-1, keepdims=True), alpha * acc + _dot(p.astype(BF16), v))

    s_t = _dot_nt(q, k_ref[seq:, :]) + bias_ref[...]
    carry = update((jnp.full((bq, 1), -jnp.inf, F32), jnp.zeros((bq, 1), F32), jnp.zeros((bq, V_HEAD), F32)),
                   s_t, v_ref[seq:, :])
    for c in range(seq // bk):
        carry = update(carry, _dot_nt(q, k_ref[c * bk:(c + 1) * bk, :]), v_ref[c * bk:(c + 1) * bk, :])
    _, den, acc = carry
    o_ref[...] = (acc / den).astype(o_ref.dtype)


def _attention(q, k, v, bias, seq, bq):
    b, h, l, _ = k.shape
    return pl.pallas_call(
        functools.partial(_attn_kernel, seq=seq, bk=_pick(seq, (1024, 512, 256, 128))), grid=(b, h, seq // bq),
        in_specs=[pl.BlockSpec((None, None, bq, QK_PAD), lambda bi, hi, i: (bi, hi, i, 0)),
                  pl.BlockSpec((None, None, l, QK_PAD), lambda bi, hi, i: (bi, hi, 0, 0)),
                  pl.BlockSpec((None, None, l, V_HEAD), lambda bi, hi, i: (bi, hi, 0, 0)),
                  pl.BlockSpec((1, TAIL), lambda bi, hi, i: (0, 0))],
        out_specs=pl.BlockSpec((None, bq, V_HEAD), lambda bi, hi, i: (bi, i, hi)),
        out_shape=jax.ShapeDtypeStruct((b, seq, h * V_HEAD), BF16),
        compiler_params=_cp(("parallel", "parallel", "arbitrary")), name="mla_attn")(q, k, v, bias)


def _merge_kernel(yf_ref, yb_ref, bonus_ref, g_ref, lng_ref, lnb_ref, red_ref, spread_ref, ym_ref, gr_ref, gm_ref,
                  bgr_ref, bgm_ref, pr_ref, pm_ref, o_ref, yr_ref):
    @pl.when(pl.program_id(2) == 0)
    def _():
        y = yf_ref[...] + yb_ref[...]
        red, spread = red_ref[...], spread_ref[...]
        d = y - _seg_sum(y, red, spread)
        var = _seg_sum(d * d, red, spread)
        yn = d * lax.rsqrt(var + GN_EPS) * lng_ref[...] + lnb_ref[...]
        yr_ref[...] = ((yn + bonus_ref[...]) * g_ref[...]).astype(BF16)

    a_r = _dot(yr_ref[...], pr_ref[...])
    a_m = _dot(ym_ref[...], pm_ref[...])
    merged = jax.nn.sigmoid(gr_ref[...] + bgr_ref[...]) * a_r + jax.nn.sigmoid(gm_ref[...] + bgm_ref[...]) * a_m
    o_ref[...] = merged.astype(o_ref.dtype)


def _merge(yf, yb, bonus, g, ln_g, ln_b, red, spread, ym, pg, b_gate, p_rwkv, p_mla, seq, tm, tn):
    b = yf.shape[0]
    c = RWKV_DIM
    d = p_rwkv.shape[1]
    nj = d // tn
    rowc = pl.BlockSpec((None, tm, c), lambda bi, i, j: (bi, i, 0))
    vec = pl.BlockSpec((1, c), lambda bi, i, j: (0, 0))
    return pl.pallas_call(
        _merge_kernel, grid=(b, seq // tm, nj),
        in_specs=[rowc, rowc, rowc, rowc, vec, vec,
                  pl.BlockSpec(red.shape, lambda bi, i, j: (0, 0)),
                  pl.BlockSpec(spread.shape, lambda bi, i, j: (0, 0)),
                  pl.BlockSpec((None, tm, ym.shape[2]), lambda bi, i, j: (bi, i, 0)),
                  pl.BlockSpec((None, tm, tn), lambda bi, i, j: (bi, i, j)),
                  pl.BlockSpec((None, tm, tn), lambda bi, i, j: (bi, i, j + nj)),
                  pl.BlockSpec((1, tn), lambda bi, i, j: (0, j)),
                  pl.BlockSpec((1, tn), lambda bi, i, j: (0, j + nj)),
                  pl.BlockSpec((c, tn), lambda bi, i, j: (0, j)),
                  pl.BlockSpec((p_mla.shape[0], tn), lambda bi, i, j: (0, j))],
        out_specs=pl.BlockSpec((None, tm, tn), lambda bi, i, j: (bi, i, j)),
        out_shape=jax.ShapeDtypeStruct((b, seq, d), BF16),
        scratch_shapes=[pltpu.VMEM((tm, c), BF16)],
        compiler_params=_cp(("parallel", "parallel", "arbitrary")), name="merge",
    )(yf, yb, bonus, g, ln_g, ln_b, red, spread, ym, pg, pg, b_gate, b_gate, p_rwkv, p_mla)


def _out_proj_kernel(a_ref, w_ref, h_ref, o_ref):
    o_ref[...] = h_ref[...] + _dot(a_ref[...], w_ref[...])


def _out_proj(merged, w_o, hp, seq, tm, tn):
    b, _, d = merged.shape
    return pl.pallas_call(
        _out_proj_kernel, grid=(b, seq // tm, d // tn),
        in_specs=[pl.BlockSpec((None, tm, d), lambda bi, i, j: (bi, i, 0)),
                  pl.BlockSpec((d, tn), lambda bi, i, j: (0, j)),
                  pl.BlockSpec((None, tm, tn), lambda bi, i, j: (bi, i, j))],
        out_specs=pl.BlockSpec((None, tm, tn), lambda bi, i, j: (bi, i, j)),
        out_shape=jax.ShapeDtypeStruct((b, seq, d), F32),
        compiler_params=_cp(("parallel", "parallel", "arbitrary")), name="out_proj")(merged, w_o, hp)


def _topk_rows(s, order, ids, k):
    vals, sel = [], []
    for _ in range(k):
        m = jnp.max(s, axis=0, keepdims=True)
        first = jnp.min(jnp.where(s == m, order, jnp.int32(2 ** 30)), axis=0, keepdims=True)
        hit = order == first
        vals.append(m)
        sel.append(first if ids is None else jnp.max(jnp.where(hit, ids, -1), axis=0, keepdims=True))
        s = jnp.where(hit, -jnp.inf, s)
    return jnp.concatenate(vals, 0), jnp.concatenate(sel, 0)


def _pair_candidates(s1, i1, s2, i2):
    k = PEER_TOPK
    assert k == 16
    tm = s1.shape[1]
    sub8 = lax.broadcasted_iota(jnp.int32, (8, tm), 0)
    sub16 = lax.broadcasted_iota(jnp.int32, (16, tm), 0)
    ninf = -jnp.inf
    vals, order, eid = [], [], []

    def add(val, o, e, valid=None):
        vals.append(val if valid is None else jnp.where(valid, val, ninf))
        order.append(o)
        eid.append(e)

    add(s1[0:1] + s2, sub16, i1[0:1] * N_KEYS + i2)
    for a, nb in ((1, 8), (2, 5), (3, 4)):
        add(s1[a:a + 1] + s2[0:8], a * k + sub8, i1[a:a + 1] * N_KEYS + i2[0:8], None if nb == 8 else sub8 < nb)
    for b, amax in ((0, 7), (1, 7), (2, 4)):
        add(s1[0:8] + s2[b:b + 1], sub8 * k + b, i1[0:8] * N_KEYS + i2[b:b + 1], (sub8 >= 4) & (sub8 <= amax))
    add(s1[8:16] + s2[0:1], (sub8 + 8) * k, i1[8:16] * N_KEYS + i2[0:1])
    return jnp.concatenate(vals, 0), jnp.concatenate(order, 0), jnp.concatenate(eid, 0)


def _route_kernel(h_ref, g_ref, wq_ref, keys_ref, e_ref, gate_ref):
    nb = _norm_rows(h_ref[...], g_ref[...]).astype(BF16)
    q = _dot(nb, wq_ref[...]).astype(BF16)
    tm = q.shape[0]
    key_rows = lax.broadcasted_iota(jnp.int32, (N_KEYS, tm), 0)
    e_rows, g_rows = [], []
    for h in range(PEER_HEADS):
        tops = []
        for half in range(2):
            gidx = 2 * h + half
            s = _dot_nt(keys_ref[gidx], q[:, gidx * N_KEYS:(gidx + 1) * N_KEYS])
            tops.append(_topk_rows(s, key_rows, None, PEER_TOPK))
        (s1, i1), (s2, i2) = tops
        best, experts = _topk_rows(*_pair_candidates(s1, i1, s2, i2), PEER_TOPK)
        ex = jnp.exp(best - best[0:1])
        g_rows.append(ex / jnp.sum(ex, axis=0, keepdims=True))
        e_rows.append(experts)
    e_ref[...] = jnp.concatenate(e_rows, 0).T
    gate_ref[...] = jnp.concatenate(g_rows, 0).T


def _route(h2, g, wq, keys, tm):
    t, d = h2.shape
    return pl.pallas_call(
        _route_kernel, grid=(t // tm,),
        in_specs=[pl.BlockSpec((tm, d), lambda i: (i, 0)),
                  pl.BlockSpec((1, d), lambda i: (0, 0)),
                  pl.BlockSpec(wq.shape, lambda i: (0, 0)),
                  pl.BlockSpec(keys.shape, lambda i: (0, 0, 0))],
        out_specs=[pl.BlockSpec((tm, PEER_SEL), lambda i: (i, 0))] * 2,
        out_shape=[jax.ShapeDtypeStruct((t, PEER_SEL), jnp.int32), jax.ShapeDtypeStruct((t, PEER_SEL), F32)],
        compiler_params=_cp(("parallel",)), name="peer_route")(h2, g, wq, keys)


def _expert_kernel(idx_ref, idxn_ref, h_ref, gate_ref, gffn_ref, gfin_ref, tab_ref, o_ref, buf_ref, sem_ref):
    i = pl.program_id(0)
    n = pl.num_programs(0)
    rows = PEER_TOK * PEER_SEL

    def row_copy(src_ref, r, dst_slot):
        return pltpu.make_async_copy(tab_ref.at[src_ref[0, 0, r]], buf_ref.at[dst_slot, pl.ds(r, 1)],
                                     sem_ref.at[dst_slot])

    @pl.when(i == 0)
    def _():
        def body(r, carry):
            row_copy(idx_ref, r, 0).start()
            return carry
        lax.fori_loop(0, rows, body, 0, unroll=8)

    def wait_slot(s):
        pltpu.make_async_copy(buf_ref.at[s], buf_ref.at[s], sem_ref.at[s]).wait()

    def step(slot):
        groups = 2 * PEER_TOK
        per = rows // groups

        def issue(gi):
            for r in range(gi * per, (gi + 1) * per):
                row_copy(idxn_ref, r, 1 - slot).start(priority=r % 2)

        h = h_ref[...]
        d = h.shape[1]
        nb = _norm_rows(h, gffn_ref[...]).astype(BF16)
        gate = gate_ref[...]
        trow = lax.broadcasted_iota(jnp.int32, gate.shape, 0)
        wait_slot(slot)
        s_blocks = []
        for t in range(PEER_TOK):
            u = buf_ref[slot, pl.ds(t * PEER_SEL, PEER_SEL), pl.ds(0, d)]
            s_blocks.append(_dot_nt(nb, u.astype(BF16)))
            issue(t)
        s = jnp.concatenate(s_blocks, axis=1)
        gmat = jnp.concatenate([jnp.where(trow == t, gate, 0.0) for t in range(PEER_TOK)], axis=1)
        act = (0.5 * s * (1.0 + lax.erf(s * (2.0 ** -0.5))) * gmat).astype(BF16)
        y = h
        for t in range(PEER_TOK):
            v = buf_ref[slot, pl.ds(t * PEER_SEL, PEER_SEL), pl.ds(d, d)]
            y = y + _dot(act[:, t * PEER_SEL:(t + 1) * PEER_SEL], v.astype(BF16))
            issue(PEER_TOK + t)
        o_ref[...] = _norm_rows(y, gfin_ref[...])

        @pl.when(i == n - 1)
        def _():
            wait_slot(1 - slot)

    for parity in range(2):
        pl.when(i % 2 == parity)(functools.partial(step, parity))


def _experts(idx, h2, gate, g_ffn, g_fin, table):
    t, d = h2.shape
    nt = t // PEER_TOK
    rows = PEER_TOK * PEER_SEL
    idx3 = idx.reshape(nt, 1, rows)
    return pl.pallas_call(
        _expert_kernel, grid=(nt,),
        in_specs=[pl.BlockSpec((1, 1, rows), lambda i: (i, 0, 0), memory_space=pltpu.SMEM),
                  pl.BlockSpec((1, 1, rows), lambda i: (jnp.minimum(i + 1, nt - 1), 0, 0), memory_space=pltpu.SMEM),
                  pl.BlockSpec((PEER_TOK, d), lambda i: (i, 0)),
                  pl.BlockSpec((PEER_TOK, PEER_SEL), lambda i: (i, 0)),
                  pl.BlockSpec((1, d), lambda i: (0, 0)),
                  pl.BlockSpec((1, d), lambda i: (0, 0)),
                  pl.BlockSpec(memory_space=pl.ANY)],
        out_specs=pl.BlockSpec((PEER_TOK, d), lambda i: (i, 0)),
        out_shape=jax.ShapeDtypeStruct((t, d), F32),
        scratch_shapes=[pltpu.VMEM((2, rows, 2 * d), F32), pltpu.SemaphoreType.DMA((2,))],
        compiler_params=_cp(("arbitrary",)), name="peer_experts")(idx3, idx3, h2, gate, g_ffn, g_fin, table)


def _segment_mats(n, blk, val):
    hit = (jnp.arange(n) // blk)[:, None] == jnp.arange(128)[None, :]
    return jnp.where(hit, val, 0.0).astype(BF16), hit.T.astype(BF16)


def _layer(hp, seq, prm, table):
    b, l, d = hp.shape
    c = RWKV_DIM
    bf = lambda a: a.astype(BF16)
    w_in = prm["w_in"]
    o_q = c * 3 + GATE_LORA + 4 * LORA
    q_lora = prm["q_norm_g"].shape[0]
    kv_lora = prm["kv_norm_g"].shape[0]
    o_kr = o_q + q_lora + kv_lora
    o_g = o_kr + QK_ROPE

    def rw_cols(a):
        lead = a.shape[:-1]
        return jnp.concatenate([a[..., :3 * c], a[..., 3 * c + GATE_LORA:o_q], a[..., 3 * c:3 * c + GATE_LORA],
                                jnp.zeros(lead + (2 * QK_NOPE - GATE_LORA,), a.dtype)], -1)

    half = QK_ROPE // 2
    swap = lambda a: jnp.concatenate([a[..., half:], a[..., :half]], -1)
    padk = lambda a: jnp.concatenate([a, jnp.zeros(a.shape[:-1] + (QK_NOPE - a.shape[-1],), a.dtype)], -1)
    w_kr = w_in[:, o_kr:o_g]
    w_rw = bf(rw_cols(w_in))
    w_mla = bf(jnp.concatenate([w_in[:, o_q:o_kr], padk(w_kr), padk(swap(w_kr))], -1))
    w_gate = bf(w_in[:, o_g:])
    g_mix = prm["norm_mix_g"][None]

    tm_all = _pick(l, (384, 128))
    tm_in = _pick(l, (1408, 384, 128))
    pr = _norm_mm(hp, g_mix, w_rw, tm_in, _pick(w_rw.shape[1], (512, 256, 128)), name="in_rwkv")
    pm = _norm_mm(hp, g_mix, w_mla, tm_in, _pick(w_mla.shape[1], (640, 128)), name="in_mla")
    pg = _norm_mm(hp, g_mix, w_gate, tm_in, 512, name="in_gate")

    zl = jnp.zeros((LORA, c), F32)
    w_lora = bf(jnp.concatenate([
        jnp.concatenate([prm["w_up"][0], zl, zl, zl], 1), jnp.concatenate([zl, prm["w_up"][1], zl, zl], 1),
        jnp.concatenate([zl, zl, prm["a_up"][0], zl], 1), jnp.concatenate([zl, zl, zl, prm["a_up"][1]], 1)], 0))
    g_up = bf(jnp.concatenate([prm["g_up"], jnp.zeros((2 * QK_NOPE - GATE_LORA, c), F32)], 0))
    red_one, spread = _segment_mats(c, RWKV_HEAD, 1.0)
    r, v, kk, lwf, lwb, ktf, ktb, bfw, bbw, g, bonus = _rwkv_prep(
        pr, rw_cols(prm["shift_c"]), w_lora, prm["w0"], prm["a0"], g_up, prm["k_k"][None], prm["k_a"][None],
        prm["r_k"].reshape(1, c), red_one, spread, seq, 128)
    t_i = np.arange(CHUNK)
    tri_f = jnp.asarray(t_i[:, None] >= t_i[None, :], BF16)
    tri_b = jnp.asarray(t_i[:, None] <= t_i[None, :], BF16)
    y_f = _scan(r, lwf, ktf, v, kk, bfw, tri_f, seq, False)
    y_b = _scan(r, lwb, ktb, v, kk, bbw, tri_b, seq, True)

    mem = np.arange(l)
    pos = np.where(mem < seq, mem + N_META, np.maximum(mem - seq - PAD_ROWS, 0)).astype(np.float32)
    inv_freq = ROPE_THETA ** (-jnp.arange(0, QK_ROPE, 2, dtype=F32) / QK_ROPE)
    ang = jnp.asarray(pos)[:, None] * inv_freq[None, :]
    cos, sin = jnp.cos(ang), jnp.sin(ang)
    z2 = jnp.zeros((l, QK_NOPE - QK_ROPE), F32)
    cs = jnp.concatenate([cos, cos, z2], 1)
    sn = jnp.concatenate([-sin, sin, z2], 1)
    qh = prm["w_uq"].reshape(q_lora, MLA_HEADS, QK_NOPE + QK_ROPE)
    heads = lambda a: a.reshape(a.shape[0], -1)
    wq = bf(jnp.concatenate([heads(qh[..., :QK_NOPE]), heads(padk(qh[..., QK_NOPE:])),
                             heads(padk(swap(qh[..., QK_NOPE:])))], -1))
    kvh = prm["w_ukv"].reshape(kv_lora, MLA_HEADS, QK_NOPE + V_HEAD)
    wkv = bf(jnp.concatenate([heads(kvh[..., :QK_NOPE]), heads(kvh[..., QK_NOPE:])], -1))
    tm_x = _pick(seq, (512, 256, 128))
    scale = (QK_NOPE + QK_ROPE) ** -0.5 * math.log2(math.e)
    q = _q_proj(pm, prm["q_norm_g"][None], wq, cs, sn, seq, tm_x, scale)
    k, vv = _kv_proj(pm, prm["kv_norm_g"][None], wkv, cs, sn, tm_all)
    bias = jnp.asarray(np.where(np.arange(TAIL) < PAD_ROWS, -1e30, 0.0)[None], F32)
    y_m = _attention(q, k, vv, bias, seq, _pick(seq, (1024, 512, 256, 128)))

    red_avg, _ = _segment_mats(c, RWKV_HEAD, 1.0 / RWKV_HEAD)
    merged = _merge(y_f, y_b, bonus, g, prm["ln_x_g"][None], prm["ln_x_b"][None], red_avg, spread, y_m, pg,
                    prm["b_gate"][None], bf(prm["p_rwkv"]), bf(prm["p_mla"]), seq, tm_x, 512)
    h2 = _out_proj(merged, bf(prm["w_o"]), hp, seq, tm_x, 512).reshape(b * seq, d)

    keys = bf(prm["peer_keys"].reshape(PEER_HEADS * 2, N_KEYS, -1))
    g_ffn = prm["norm_ffn_g"][None]
    experts, gates = _route(h2, g_ffn, bf(prm["peer_wq"]), keys, _pick(b * seq, (256, 128)))
    return experts, gates, h2, g_ffn


def kernel(x, meta_tokens, norm_mix_g, w_in, b_gate, shift_c, w_up, w0, a_up, a0, g_up, k_k, k_a, r_k,
           ln_x_g, ln_x_b, q_norm_g, w_uq, kv_norm_g, w_ukv, p_rwkv, p_mla, w_o, norm_ffn_g,
           peer_wq, peer_keys, peer_u, peer_v, final_norm_g):
    b, seq, d = x.shape
    assert norm_mix_g.shape[0] == 1, "single-layer block"
    assert seq % 128 == 0 and meta_tokens.shape[0] == N_META
    meta = jnp.broadcast_to(meta_tokens.astype(x.dtype)[None], (b, N_META, d))
    hp = jnp.concatenate([x, jnp.zeros((b, PAD_ROWS, d), x.dtype), meta], axis=1)
    prm = dict(norm_mix_g=norm_mix_g[0], w_in=w_in[0], b_gate=b_gate[0], shift_c=shift_c[0], w_up=w_up[0],
               w0=w0[0], a_up=a_up[0], a0=a0[0], g_up=g_up[0], k_k=k_k[0], k_a=k_a[0], r_k=r_k[0],
               ln_x_g=ln_x_g[0], ln_x_b=ln_x_b[0], q_norm_g=q_norm_g[0], w_uq=w_uq[0], kv_norm_g=kv_norm_g[0],
               w_ukv=w_ukv[0], p_rwkv=p_rwkv[0], p_mla=p_mla[0], w_o=w_o[0], norm_ffn_g=norm_ffn_g[0],
               peer_wq=peer_wq[0], peer_keys=peer_keys[0])
    table = _expert_table(peer_u[0], peer_v[0])
    experts, gates, h2, g_ffn = _layer(hp, seq, prm, table)
    out = _experts(experts, h2, gates, g_ffn, final_norm_g[None], table)
    return out.reshape(b, seq, d)
```

```python
import functools
import math

import jax
import jax.numpy as jnp
import numpy as np
from jax import lax
from jax.experimental import pallas as pl
from jax.experimental.pallas import tpu as pltpu

F32 = jnp.float32
BF16 = jnp.bfloat16

N_META = 16
PAD_ROWS = 112
TAIL = PAD_ROWS + N_META
NORM_EPS = 1e-6
GN_EPS = 64e-5
ROPE_THETA = 10000.0

RWKV_HEADS = 16
RWKV_HEAD = 64
RWKV_DIM = RWKV_HEADS * RWKV_HEAD
GATE_LORA = 160
LORA = 64
CHUNK = 64

MLA_HEADS = 16
QK_NOPE = 128
QK_ROPE = 64
V_HEAD = 128
QK_PAD = 256

N_KEYS = 128
PEER_HEADS = 8
PEER_TOPK = 16
PEER_SEL = PEER_HEADS * PEER_TOPK
PEER_TOK = 8

VMEM_LIMIT = 56 * 1024 * 1024


def _cp(sem, vmem=VMEM_LIMIT):
    return pltpu.CompilerParams(dimension_semantics=sem, vmem_limit_bytes=vmem)


def _pick(n, cands):
    for c in cands:
        if n % c == 0:
            return c
    raise ValueError(f"no tile for {n}")


def _dot(a, b):
    return jnp.dot(a, b, preferred_element_type=F32)


def _dot_nt(a, b):
    return lax.dot_general(a, b, (((1,), (1,)), ((), ())), preferred_element_type=F32)


def _split_bf16(x):
    hi = x.astype(BF16)
    lo = (x - hi.astype(F32)).astype(BF16)
    return hi, lo


def _dot2(x, w):
    hi, lo = _split_bf16(x)
    return _dot(hi, w) + _dot(lo, w)


def _table_kernel(u_ref, v_ref, o_ref):
    o_ref[...] = jnp.concatenate([u_ref[...], v_ref[...]], axis=1)[:, None, :]


def _expert_table(u, v):
    n, d = u.shape
    tr = _pick(n, (256, 128, 8))
    spec = pl.BlockSpec((tr, d), lambda i: (i, 0))
    return pl.pallas_call(
        _table_kernel, grid=(n // tr,), in_specs=[spec, spec],
        out_specs=pl.BlockSpec((tr, 1, 2 * d), lambda i: (i, 0, 0)),
        out_shape=jax.ShapeDtypeStruct((n, 1, 2 * d), u.dtype),
        compiler_params=_cp(("parallel",)), name="peer_table")(u, v)


def _seg_sum(x, red, spread):
    return _dot2(_dot2(x, red), spread)


def _norm_rows(x, g):
    ms = jnp.mean(x * x, axis=-1, keepdims=True)
    return x * lax.rsqrt(ms + NORM_EPS) * g


def _norm_mm_kernel(x_ref, g_ref, w_ref, o_ref, nb_ref):
    @pl.when(pl.program_id(2) == 0)
    def _():
        nb_ref[...] = _norm_rows(x_ref[...], g_ref[...]).astype(BF16)

    o_ref[...] = _dot(nb_ref[...], w_ref[...]).astype(o_ref.dtype)


def _norm_mm(x, g, w, tm, tn, out_dtype=F32, name="norm_mm"):
    b, l, k = x.shape
    n = w.shape[1]
    return pl.pallas_call(
        _norm_mm_kernel, grid=(b, l // tm, n // tn),
        in_specs=[pl.BlockSpec((None, tm, k), lambda bi, i, j: (bi, i, 0)),
                  pl.BlockSpec((1, k), lambda bi, i, j: (0, 0)),
                  pl.BlockSpec((k, tn), lambda bi, i, j: (0, j))],
        out_specs=pl.BlockSpec((None, tm, tn), lambda bi, i, j: (bi, i, j)),
        out_shape=jax.ShapeDtypeStruct((b, l, n), out_dtype),
        scratch_shapes=[pltpu.VMEM((tm, k), BF16)],
        compiler_params=_cp(("parallel", "parallel", "arbitrary")), name=name)(x, g, w)


def _rwkv_prep_kernel(p_ref, pp_ref, pn_ref, sc_ref, wl_ref, w0_ref, a0_ref, gup_ref, kkw_ref, ka_ref, rk_ref,
                      red_ref, spread_ref, r_ref, v_ref, kk_ref, lwf_ref, lwb_ref, ktf_ref, ktb_ref, bf_ref, bb_ref,
                      g_ref, bonus_ref, *, seq, tm):
    p = p_ref[...]
    row = lax.broadcasted_iota(jnp.int32, p.shape, 0)
    prev = jnp.where(row == 0, pp_ref[7:8, :], pltpu.roll(p, 1, 0))
    nxt = jnp.where(row == tm - 1, pn_ref[0:1, :], pltpu.roll(p, tm - 1, 0))
    xs = sc_ref[0:1, :] * prev + sc_ref[1:2, :] * p + sc_ref[2:3, :] * nxt
    c = RWKV_DIM
    r, k, v = xs[:, :c], xs[:, c:2 * c], xs[:, 2 * c:3 * c]
    lora = xs[:, 3 * c:3 * c + 4 * LORA]
    lane = lax.broadcasted_iota(jnp.int32, lora.shape, 1)
    lora = jnp.where(lane < 2 * LORA, jnp.tanh(lora), lora)
    lo = _dot(lora.astype(BF16), wl_ref[...])
    g = _dot(jax.nn.sigmoid(xs[:, 3 * c + 4 * LORA:]).astype(BF16), gup_ref[...])
    red, spread = red_ref[...], spread_ref[...]
    kk = k * kkw_ref[...]
    nrm = jnp.sqrt(_seg_sum(kk * kk, red, spread))
    kk = kk / jnp.maximum(nrm, 1e-12)
    grow = pl.program_id(1) * tm + lax.broadcasted_iota(jnp.int32, v.shape, 0)
    v = jnp.where((grow < seq) | (grow >= seq + PAD_ROWS), v, 0.0)
    ka = ka_ref[...]
    kts = []
    for d, (lw_ref, kt_ref, b_ref) in enumerate(((lwf_ref, ktf_ref, bf_ref), (lwb_ref, ktb_ref, bb_ref))):
        w_log = -jax.nn.softplus(-(w0_ref[d:d + 1, :] + lo[:, d * c:(d + 1) * c])) - 0.5
        lw_ref[...] = -jnp.exp(w_log)
        iclr = jax.nn.sigmoid(a0_ref[d:d + 1, :] + lo[:, (2 + d) * c:(3 + d) * c])
        kt = k * (1.0 + (iclr - 1.0) * ka)
        kt_ref[...] = kt
        b_ref[...] = kk * iclr
        kts.append(kt)
    r_ref[...] = r
    v_ref[...] = v
    kk_ref[...] = kk
    g_ref[...] = g
    bonus_ref[...] = _seg_sum(r * (0.5 * (kts[0] + kts[1])) * rk_ref[...], red, spread) * v


def _rwkv_prep(pr, shift_c, w_lora, w0, a0, g_up, k_k, k_a, r_k, red, spread, seq, tm):
    b, l, cols = pr.shape
    c = RWKV_DIM
    nb8 = l // 8
    row = lambda bi, i: (bi, i, 0)
    full = lambda a: pl.BlockSpec(a.shape, lambda bi, i: (0,) * a.ndim)
    out = jax.ShapeDtypeStruct((b, l, c), F32)
    ospec = pl.BlockSpec((None, tm, c), row)
    return pl.pallas_call(
        functools.partial(_rwkv_prep_kernel, seq=seq, tm=tm), grid=(b, l // tm),
        in_specs=[pl.BlockSpec((None, tm, cols), row),
                  pl.BlockSpec((None, 8, cols), lambda bi, i: (bi, (i * (tm // 8) + nb8 - 1) % nb8, 0)),
                  pl.BlockSpec((None, 8, cols), lambda bi, i: (bi, ((i + 1) * (tm // 8)) % nb8, 0)),
                  full(shift_c), full(w_lora), full(w0), full(a0), full(g_up), full(k_k), full(k_a), full(r_k),
                  full(red), full(spread)],
        out_specs=[ospec] * 11, out_shape=[out] * 11,
        compiler_params=_cp(("parallel", "parallel")), name="rwkv_prep",
    )(pr, pr, pr, shift_c, w_lora, w0, a0, g_up, k_k, k_a, r_k, red, spread)


def _scan_kernel(r_ref, lw_ref, kt_ref, v_ref, kk_ref, b_ref, tri_ref, y_ref, ht_ref, *, reverse):
    @pl.when(pl.program_id(1) == 0)
    def _():
        ht_ref[...] = jnp.zeros_like(ht_ref)

    lw = lw_ref[...]
    hi, lo = _split_bf16(lw)
    tri = tri_ref[...]
    cum = _dot(tri, hi) + _dot(tri, lo)
    tot = cum[0:1] if reverse else cum[CHUNK - 1:CHUNK]
    e_in = jnp.exp(cum)
    e_inv = jnp.exp(-cum)
    e_ex = jnp.exp(cum - lw)
    e_tot = jnp.exp(tot)
    at_all = -(kk_ref[...] * e_ex)
    rt_all = r_ref[...] * e_in
    bt_all = b_ref[...] * e_inv
    kt_all = kt_ref[...] * e_inv
    v_all = v_ref[...]

    n2 = 2 * CHUNK
    ri = lax.broadcasted_iota(jnp.int32, (n2, n2), 0)
    ci = lax.broadcasted_iota(jnp.int32, (n2, n2), 1)
    tt, ss = ri % CHUNK, ci % CHUNK
    strict = (tt < ss) if reverse else (tt > ss)
    incl = (tt <= ss) if reverse else (tt >= ss)
    bd = (ri < CHUNK) == (ci < CHUNK)
    eye = (ri == ci).astype(F32)
    head0 = lax.broadcasted_iota(jnp.int32, (CHUNK, n2), 1) < RWKV_HEAD
    cat = jnp.concatenate
    bf = lambda t: t.astype(BF16)

    pairs = range(RWKV_HEADS // 2)
    sls = [slice(p * n2, (p + 1) * n2) for p in pairs]
    h_old = [ht_ref[p] for p in pairs]
    at = [at_all[:, sl] for sl in sls]
    rt = [rt_all[:, sl] for sl in sls]
    bt = [bt_all[:, sl] for sl in sls]
    kt = [kt_all[:, sl] for sl in sls]
    v = [v_all[:, sl] for sl in sls]
    at0 = [jnp.where(head0, t, 0.0) for t in at]
    rt0 = [jnp.where(head0, t, 0.0) for t in rt]
    v0 = [jnp.where(head0, t, 0.0) for t in v]
    g0 = [_dot_nt(bf(cat([at0[p], rt0[p]], 0)), bf(cat([bt[p], kt[p]], 0))) for p in pairs]
    g1 = [_dot_nt(bf(cat([at[p] - at0[p], rt[p] - rt0[p]], 0)), bf(cat([kt[p], bt[p]], 0))) for p in pairs]
    ga = [jnp.where(strict, cat([g0[p][:CHUNK], g1[p][:CHUNK]], 0), 0.0) for p in pairs]
    gr = [jnp.where(incl, cat([g0[p][CHUNK:], g1[p][CHUNK:]], 0), 0.0) for p in pairs]
    pk = [jnp.where(bd, t, 0.0) for t in ga]
    ga_anti = [bf(ga[p] - pk[p]) for p in pairs]
    tinv = [eye + t for t in pk]
    for _ in range(int(math.log2(CHUNK)) - 1):
        pk = [_dot(bf(t), bf(t)) for t in pk]
        tinv = [tinv[p] + _dot(bf(tinv[p]), bf(pk[p])) for p in pairs]
    ht = [bf(t) for t in h_old]
    vx = [bf(cat([v[p] - v0[p], v0[p]], 0)) for p in pairs]
    ws = [_dot_nt(bf(cat([at0[p], at[p] - at0[p]], 0)), ht[p]) + _dot(ga_anti[p], vx[p]) for p in pairs]
    us = [_dot(bf(tinv[p]), bf(ws[p])) for p in pairs]
    gr_d = [jnp.where(bd, t, 0.0) for t in gr]
    ys = [_dot_nt(bf(cat([rt0[p], rt[p] - rt0[p]], 0)), ht[p])
          + _dot(bf(cat([gr_d[p], gr[p] - gr_d[p]], 1)), cat([bf(us[p]), vx[p]], 0)) for p in pairs]
    et = [e_tot[:, sl] for sl in sls]
    hn = [_dot(bf(cat([us[p][:CHUNK] + us[p][CHUNK:], v[p]], 0).T), bf(cat([bt[p] * et[p], kt[p] * et[p]], 0)))
          for p in pairs]
    for p in pairs:
        y_ref[:, sls[p]] = ys[p][:CHUNK] + ys[p][CHUNK:]
        ht_ref[p] = et[p] * h_old[p] + jnp.where(bd, hn[p], 0.0)


def _scan_both_kernel(rf, lwf, ktf, vf, kkf, bff, trif, rb, lwb, ktb, vb, kkb, bbb, trib, yf, yb, htf, htb):
    _scan_kernel(rf, lwf, ktf, vf, kkf, bff, trif, yf, htf, reverse=False)
    _scan_kernel(rb, lwb, ktb, vb, kkb, bbb, trib, yb, htb, reverse=True)


def _scan_both(r, v, kk, lwf, ktf, bfw, lwb, ktb, bbw, tri_f, tri_b, seq):
    b, l, c = r.shape
    nmem = l // CHUNK
    nx = seq // CHUNK
    fmap = lambda bi, ci: (bi, (ci + nmem - 1) % nmem, 0)
    bmap = lambda bi, ci: (bi, jnp.where(ci < nx, nx - 1 - ci, nx + nmem - 1 - ci), 0)
    fspec = pl.BlockSpec((None, CHUNK, c), fmap)
    bspec = pl.BlockSpec((None, CHUNK, c), bmap)
    tspec = pl.BlockSpec((CHUNK, CHUNK), lambda bi, ci: (0, 0))
    state = pltpu.VMEM((RWKV_HEADS // 2, 2 * CHUNK, 2 * CHUNK), F32)
    out = jax.ShapeDtypeStruct((b, l, c), F32)
    return pl.pallas_call(
        _scan_both_kernel, grid=(b, nmem),
        in_specs=[fspec] * 6 + [tspec] + [bspec] * 6 + [tspec],
        out_specs=[fspec, bspec], out_shape=[out, out], scratch_shapes=[state, state],
        compiler_params=_cp(("parallel", "arbitrary")), name="wkv_both",
    )(r, lwf, ktf, v, kk, bfw, tri_f, r, lwb, ktb, v, kk, bbw, tri_b)


def _scan(r, lw, kt, v, kk, bb, tri, seq, reverse):
    b, l, c = r.shape
    nmem = l // CHUNK
    nx = seq // CHUNK
    nc = nmem
    if reverse:
        cmap = lambda bi, ci: (bi, jnp.where(ci < nx, nx - 1 - ci, nx + nmem - 1 - ci), 0)
    else:
        cmap = lambda bi, ci: (bi, (ci + nmem - 1) % nmem, 0)
    spec = pl.BlockSpec((None, CHUNK, c), cmap)
    return pl.pallas_call(
        functools.partial(_scan_kernel, reverse=reverse), grid=(b, nc),
        in_specs=[spec] * 6 + [pl.BlockSpec((CHUNK, CHUNK), lambda bi, ci: (0, 0))],
        out_specs=spec, out_shape=jax.ShapeDtypeStruct((b, l, c), F32),
        scratch_shapes=[pltpu.VMEM((RWKV_HEADS // 2, 2 * CHUNK, 2 * CHUNK), F32)],
        compiler_params=_cp(("parallel", "arbitrary")), name="wkv_bwd" if reverse else "wkv_fwd",
    )(r, lw, kt, v, kk, bb, tri)


def _q_proj_kernel(x_ref, g_ref, w_ref, cs_ref, sn_ref, o_ref, *, scale):
    acc = _dot(_norm_rows(x_ref[...], g_ref[...]).astype(BF16), w_ref[...])
    cs = cs_ref[...] * scale
    sn = sn_ref[...] * scale
    hw = MLA_HEADS * QK_NOPE
    for h in range(MLA_HEADS):
        lo = h * QK_NOPE
        o_ref[h, :, :QK_NOPE] = (acc[:, lo:lo + QK_NOPE] * scale).astype(o_ref.dtype)
        rot = acc[:, hw + lo:hw + lo + QK_NOPE] * cs + acc[:, 2 * hw + lo:2 * hw + lo + QK_NOPE] * sn
        o_ref[h, :, QK_NOPE:] = rot.astype(o_ref.dtype)


def _q_proj(pm, g, wq, cs, sn, seq, tm, scale):
    b = pm.shape[0]
    k = g.shape[1]
    return pl.pallas_call(
        functools.partial(_q_proj_kernel, scale=scale), grid=(b, seq // tm),
        in_specs=[pl.BlockSpec((None, tm, k), lambda bi, i: (bi, i, 0)),
                  pl.BlockSpec((1, k), lambda bi, i: (0, 0)),
                  pl.BlockSpec(wq.shape, lambda bi, i: (0, 0)),
                  pl.BlockSpec((tm, QK_NOPE), lambda bi, i: (i, 0)),
                  pl.BlockSpec((tm, QK_NOPE), lambda bi, i: (i, 0))],
        out_specs=pl.BlockSpec((None, MLA_HEADS, tm, QK_PAD), lambda bi, i: (bi, 0, i, 0)),
        out_shape=jax.ShapeDtypeStruct((b, MLA_HEADS, seq, QK_PAD), BF16),
        compiler_params=_cp(("parallel", "parallel")), name="mla_q")(pm, g, wq, cs, sn)


def _kv_proj_kernel(x_ref, g_ref, w_ref, ka_ref, kb_ref, cs_ref, sn_ref, k_ref, v_ref):
    acc = _dot(_norm_rows(x_ref[...], g_ref[...]).astype(BF16), w_ref[...])
    rope = (ka_ref[...] * cs_ref[...] + kb_ref[...] * sn_ref[...]).astype(k_ref.dtype)
    hw = MLA_HEADS * QK_NOPE
    for h in range(MLA_HEADS):
        lo = h * QK_NOPE
        k_ref[h, :, :QK_NOPE] = acc[:, lo:lo + QK_NOPE].astype(k_ref.dtype)
        k_ref[h, :, QK_NOPE:] = rope
        v_ref[h] = acc[:, hw + lo:hw + lo + V_HEAD].astype(v_ref.dtype)


def _kv_proj(pm, g, wkv, cs, sn, tm):
    b, l, _ = pm.shape
    k = g.shape[1]
    kb = k // QK_NOPE
    return pl.pallas_call(
        _kv_proj_kernel, grid=(b, l // tm),
        in_specs=[pl.BlockSpec((None, tm, k), lambda bi, i: (bi, i, 1)),
                  pl.BlockSpec((1, k), lambda bi, i: (0, 0)),
                  pl.BlockSpec(wkv.shape, lambda bi, i: (0, 0)),
                  pl.BlockSpec((None, tm, QK_NOPE), lambda bi, i: (bi, i, 2 * kb)),
                  pl.BlockSpec((None, tm, QK_NOPE), lambda bi, i: (bi, i, 2 * kb + 1)),
                  pl.BlockSpec((tm, QK_NOPE), lambda bi, i: (i, 0)),
                  pl.BlockSpec((tm, QK_NOPE), lambda bi, i: (i, 0))],
        out_specs=[pl.BlockSpec((None, MLA_HEADS, tm, QK_PAD), lambda bi, i: (bi, 0, i, 0)),
                   pl.BlockSpec((None, MLA_HEADS, tm, V_HEAD), lambda bi, i: (bi, 0, i, 0))],
        out_shape=[jax.ShapeDtypeStruct((b, MLA_HEADS, l, QK_PAD), BF16),
                   jax.ShapeDtypeStruct((b, MLA_HEADS, l, V_HEAD), BF16)],
        compiler_params=_cp(("parallel", "parallel")), name="mla_kv")(pm, g, wkv, pm, pm, cs, sn)


def _attn_kernel(q_ref, k_ref, v_ref, bias_ref, o_ref, *, seq, bk):
    q = q_ref[...]
    bq = q.shape[0]

    def update(carry, s, v):
        m, den, acc = carry
        m_new = jnp.maximum(m, jnp.max(s, axis=-1, keepdims=True))
        alpha = jnp.exp2(m - m_new)
        p = jnp.exp2(s - m_new)
        return (m_new, alpha * den + jnp.sum(p, axis=-1, keepdims=True), alpha * acc + _dot(p.astype(BF16), v))

    s_t = _dot_nt(q, k_ref[seq:, :]) + bias_ref[...]
    carry = update((jnp.full((bq, 1), -jnp.inf, F32), jnp.zeros((bq, 1), F32), jnp.zeros((bq, V_HEAD), F32)),
                   s_t, v_ref[seq:, :])
    for c in range(seq // bk):
        carry = update(carry, _dot_nt(q, k_ref[c * bk:(c + 1) * bk, :]), v_ref[c * bk:(c + 1) * bk, :])
    _, den, acc = carry
    o_ref[...] = (acc / den).astype(o_ref.dtype)


def _attention(q, k, v, bias, seq, bq):
    b, h, l, _ = k.shape
    return pl.pallas_call(
        functools.partial(_attn_kernel, seq=seq, bk=_pick(seq, (1024, 512, 256, 128))), grid=(b, h, seq // bq),
        in_specs=[pl.BlockSpec((None, None, bq, QK_PAD), lambda bi, hi, i: (bi, hi, i, 0)),
                  pl.BlockSpec((None, None, l, QK_PAD), lambda bi, hi, i: (bi, hi, 0, 0)),
                  pl.BlockSpec((None, None, l, V_HEAD), lambda bi, hi, i: (bi, hi, 0, 0)),
                  pl.BlockSpec((1, TAIL), lambda bi, hi, i: (0, 0))],
        out_specs=pl.BlockSpec((None, bq, V_HEAD), lambda bi, hi, i: (bi, i, hi)),
        out_shape=jax.ShapeDtypeStruct((b, seq, h * V_HEAD), BF16),
        compiler_params=_cp(("parallel", "parallel", "arbitrary")), name="mla_attn")(q, k, v, bias)


def _merge_kernel(yf_ref, yb_ref, bonus_ref, g_ref, lng_ref, lnb_ref, red_ref, spread_ref, ym_ref, gr_ref, gm_ref,
                  bgr_ref, bgm_ref, pr_ref, pm_ref, o_ref, yr_ref):
    @pl.when(pl.program_id(2) == 0)
    def _():
        y = yf_ref[...] + yb_ref[...]
        red, spread = red_ref[...], spread_ref[...]
        d = y - _seg_sum(y, red, spread)
        var = _seg_sum(d * d, red, spread)
        yn = d * lax.rsqrt(var + GN_EPS) * lng_ref[...] + lnb_ref[...]
        yr_ref[...] = ((yn + bonus_ref[...]) * g_ref[...]).astype(BF16)

    a_r = _dot(yr_ref[...], pr_ref[...])
    a_m = _dot(ym_ref[...], pm_ref[...])
    merged = jax.nn.sigmoid(gr_ref[...] + bgr_ref[...]) * a_r + jax.nn.sigmoid(gm_ref[...] + bgm_ref[...]) * a_m
    o_ref[...] = merged.astype(o_ref.dtype)


def _merge(yf, yb, bonus, g, ln_g, ln_b, red, spread, ym, pg, b_gate, p_rwkv, p_mla, seq, tm, tn):
    b = yf.shape[0]
    c = RWKV_DIM
    d = p_rwkv.shape[1]
    nj = d // tn
    rowc = pl.BlockSpec((None, tm, c), lambda bi, i, j: (bi, i, 0))
    vec = pl.BlockSpec((1, c), lambda bi, i, j: (0, 0))
    return pl.pallas_call(
        _merge_kernel, grid=(b, seq // tm, nj),
        in_specs=[rowc, rowc, rowc, rowc, vec, vec,
                  pl.BlockSpec(red.shape, lambda bi, i, j: (0, 0)),
                  pl.BlockSpec(spread.shape, lambda bi, i, j: (0, 0)),
                  pl.BlockSpec((None, tm, ym.shape[2]), lambda bi, i, j: (bi, i, 0)),
                  pl.BlockSpec((None, tm, tn), lambda bi, i, j: (bi, i, j)),
                  pl.BlockSpec((None, tm, tn), lambda bi, i, j: (bi, i, j + nj)),
                  pl.BlockSpec((1, tn), lambda bi, i, j: (0, j)),
                  pl.BlockSpec((1, tn), lambda bi, i, j: (0, j + nj)),
                  pl.BlockSpec((c, tn), lambda bi, i, j: (0, j)),
                  pl.BlockSpec((p_mla.shape[0], tn), lambda bi, i, j: (0, j))],
        out_specs=pl.BlockSpec((None, tm, tn), lambda bi, i, j: (bi, i, j)),
        out_shape=jax.ShapeDtypeStruct((b, seq, d), BF16),
        scratch_shapes=[pltpu.VMEM((tm, c), BF16)],
        compiler_params=_cp(("parallel", "parallel", "arbitrary")), name="merge",
    )(yf, yb, bonus, g, ln_g, ln_b, red, spread, ym, pg, pg, b_gate, b_gate, p_rwkv, p_mla)


def _out_proj_kernel(a_ref, w_ref, h_ref, o_ref):
    o_ref[...] = h_ref[...] + _dot(a_ref[...], w_ref[...])


def _out_proj(merged, w_o, hp, seq, tm, tn):
    b, _, d = merged.shape
    return pl.pallas_call(
        _out_proj_kernel, grid=(b, seq // tm, d // tn),
        in_specs=[pl.BlockSpec((None, tm, d), lambda bi, i, j: (bi, i, 0)),
                  pl.BlockSpec((d, tn), lambda bi, i, j: (0, j)),
                  pl.BlockSpec((None, tm, tn), lambda bi, i, j: (bi, i, j))],
        out_specs=pl.BlockSpec((None, tm, tn), lambda bi, i, j: (bi, i, j)),
        out_shape=jax.ShapeDtypeStruct((b, seq, d), F32),
        compiler_params=_cp(("parallel", "parallel", "arbitrary")), name="out_proj")(merged, w_o, hp)


def _topk_rows(s, order, ids, k):
    vals, sel = [], []
    for _ in range(k):
        m = jnp.max(s, axis=0, keepdims=True)
        first = jnp.min(jnp.where(s == m, order, jnp.int32(2 ** 30)), axis=0, keepdims=True)
        hit = order == first
        vals.append(m)
        sel.append(first if ids is None else jnp.max(jnp.where(hit, ids, -1), axis=0, keepdims=True))
        s = jnp.where(hit, -jnp.inf, s)
    return jnp.concatenate(vals, 0), jnp.concatenate(sel, 0)


def _pair_candidates(s1, i1, s2, i2):
    k = PEER_TOPK
    assert k == 16
    tm = s1.shape[1]
    sub8 = lax.broadcasted_iota(jnp.int32, (8, tm), 0)
    sub16 = lax.broadcasted_iota(jnp.int32, (16, tm), 0)
    ninf = -jnp.inf
    vals, order, eid = [], [], []

    def add(val, o, e, valid=None):
        vals.append(val if valid is None else jnp.where(valid, val, ninf))
        order.append(o)
        eid.append(e)

    add(s1[0:1] + s2, sub16, i1[0:1] * N_KEYS + i2)
    for a, nb in ((1, 8), (2, 5), (3, 4)):
        add(s1[a:a + 1] + s2[0:8], a * k + sub8, i1[a:a + 1] * N_KEYS + i2[0:8], None if nb == 8 else sub8 < nb)
    for b, amax in ((0, 7), (1, 7), (2, 4)):
        add(s1[0:8] + s2[b:b + 1], sub8 * k + b, i1[0:8] * N_KEYS + i2[b:b + 1], (sub8 >= 4) & (sub8 <= amax))
    add(s1[8:16] + s2[0:1], (sub8 + 8) * k, i1[8:16] * N_KEYS + i2[0:1])
    return jnp.concatenate(vals, 0), jnp.concatenate(order, 0), jnp.concatenate(eid, 0)


def _route_kernel(h_ref, g_ref, wq_ref, keys_ref, e_ref, gate_ref):
    nb = _norm_rows(h_ref[...], g_ref[...]).astype(BF16)
    q = _dot(nb, wq_ref[...]).astype(BF16)
    tm = q.shape[0]
    key_rows = lax.broadcasted_iota(jnp.int32, (N_KEYS, tm), 0)
    e_rows, g_rows = [], []
    for h in range(PEER_HEADS):
        tops = []
        for half in range(2):
            gidx = 2 * h + half
            s = _dot_nt(keys_ref[gidx], q[:, gidx * N_KEYS:(gidx + 1) * N_KEYS])
            tops.append(_topk_rows(s, key_rows, None, PEER_TOPK))
        (s1, i1), (s2, i2) = tops
        best, experts = _topk_rows(*_pair_candidates(s1, i1, s2, i2), PEER_TOPK)
        ex = jnp.exp(best - best[0:1])
        g_rows.append(ex / jnp.sum(ex, axis=0, keepdims=True))
        e_rows.append(experts)
    e_ref[...] = jnp.concatenate(e_rows, 0).T
    gate_ref[...] = jnp.concatenate(g_rows, 0).T


def _route(h2, g, wq, keys, tm):
    t, d = h2.shape
    return pl.pallas_call(
        _route_kernel, grid=(t // tm,),
        in_specs=[pl.BlockSpec((tm, d), lambda i: (i, 0)),
                  pl.BlockSpec((1, d), lambda i: (0, 0)),
                  pl.BlockSpec(wq.shape, lambda i: (0, 0)),
                  pl.BlockSpec(keys.shape, lambda i: (0, 0, 0))],
        out_specs=[pl.BlockSpec((tm, PEER_SEL), lambda i: (i, 0))] * 2,
        out_shape=[jax.ShapeDtypeStruct((t, PEER_SEL), jnp.int32), jax.ShapeDtypeStruct((t, PEER_SEL), F32)],
        compiler_params=_cp(("parallel",)), name="peer_route")(h2, g, wq, keys)


def _expert_kernel(idx_ref, idxn_ref, h_ref, gate_ref, gffn_ref, gfin_ref, tab_ref, o_ref, buf_ref, sem_ref):
    i = pl.program_id(0)
    n = pl.num_programs(0)
    rows = PEER_TOK * PEER_SEL

    def row_copy(src_ref, r, dst_slot):
        return pltpu.make_async_copy(tab_ref.at[src_ref[0, 0, r]], buf_ref.at[dst_slot, pl.ds(r, 1)],
                                     sem_ref.at[dst_slot])

    @pl.when(i == 0)
    def _():
        def body(r, carry):
            row_copy(idx_ref, r, 0).start()
            return carry
        lax.fori_loop(0, rows, body, 0, unroll=8)

    def wait_slot(s):
        pltpu.make_async_copy(buf_ref.at[s], buf_ref.at[s], sem_ref.at[s]).wait()

    def step(slot):
        groups = 2 * PEER_TOK
        per = rows // groups

        def issue(gi):
            for r in range(gi * per, (gi + 1) * per):
                row_copy(idxn_ref, r, 1 - slot).start(priority=r % 2)

        h = h_ref[...]
        d = h.shape[1]
        nb = _norm_rows(h, gffn_ref[...]).astype(BF16)
        gate = gate_ref[...]
        trow = lax.broadcasted_iota(jnp.int32, gate.shape, 0)
        wait_slot(slot)
        s_blocks = []
        for t in range(PEER_TOK):
            u = buf_ref[slot, pl.ds(t * PEER_SEL, PEER_SEL), pl.ds(0, d)]
            s_blocks.append(_dot_nt(nb, u.astype(BF16)))
            issue(t)
        s = jnp.concatenate(s_blocks, axis=1)
        gmat = jnp.concatenate([jnp.where(trow == t, gate, 0.0) for t in range(PEER_TOK)], axis=1)
        act = (0.5 * s * (1.0 + lax.erf(s * (2.0 ** -0.5))) * gmat).astype(BF16)
        y = h
        for t in range(PEER_TOK):
            v = buf_ref[slot, pl.ds(t * PEER_SEL, PEER_SEL), pl.ds(d, d)]
            y = y + _dot(act[:, t * PEER_SEL:(t + 1) * PEER_SEL], v.astype(BF16))
            issue(PEER_TOK + t)
        o_ref[...] = _norm_rows(y, gfin_ref[...])

        @pl.when(i == n - 1)
        def _():
            wait_slot(1 - slot)

    for parity in range(2):
        pl.when(i % 2 == parity)(functools.partial(step, parity))


def _experts(idx, h2, gate, g_ffn, g_fin, table):
    t, d = h2.shape
    nt = t // PEER_TOK
    rows = PEER_TOK * PEER_SEL
    idx3 = idx.reshape(nt, 1, rows)
    return pl.pallas_call(
        _expert_kernel, grid=(nt,),
        in_specs=[pl.BlockSpec((1, 1, rows), lambda i: (i, 0, 0), memory_space=pltpu.SMEM),
                  pl.BlockSpec((1, 1, rows), lambda i: (jnp.minimum(i + 1, nt - 1), 0, 0), memory_space=pltpu.SMEM),
                  pl.BlockSpec((PEER_TOK, d), lambda i: (i, 0)),
                  pl.BlockSpec((PEER_TOK, PEER_SEL), lambda i: (i, 0)),
                  pl.BlockSpec((1, d), lambda i: (0, 0)),
                  pl.BlockSpec((1, d), lambda i: (0, 0)),
                  pl.BlockSpec(memory_space=pl.ANY)],
        out_specs=pl.BlockSpec((PEER_TOK, d), lambda i: (i, 0)),
        out_shape=jax.ShapeDtypeStruct((t, d), F32),
        scratch_shapes=[pltpu.VMEM((2, rows, 2 * d), F32), pltpu.SemaphoreType.DMA((2,))],
        compiler_params=_cp(("arbitrary",)), name="peer_experts")(idx3, idx3, h2, gate, g_ffn, g_fin, table)


def _segment_mats(n, blk, val):
    hit = (jnp.arange(n) // blk)[:, None] == jnp.arange(128)[None, :]
    return jnp.where(hit, val, 0.0).astype(BF16), hit.T.astype(BF16)


def _layer(hp, seq, prm, table):
    b, l, d = hp.shape
    c = RWKV_DIM
    bf = lambda a: a.astype(BF16)
    w_in = prm["w_in"]
    o_q = c * 3 + GATE_LORA + 4 * LORA
    q_lora = prm["q_norm_g"].shape[0]
    kv_lora = prm["kv_norm_g"].shape[0]
    o_kr = o_q + q_lora + kv_lora
    o_g = o_kr + QK_ROPE

    def rw_cols(a):
        lead = a.shape[:-1]
        return jnp.concatenate([a[..., :3 * c], a[..., 3 * c + GATE_LORA:o_q], a[..., 3 * c:3 * c + GATE_LORA],
                                jnp.zeros(lead + (2 * QK_NOPE - GATE_LORA,), a.dtype)], -1)

    half = QK_ROPE // 2
    swap = lambda a: jnp.concatenate([a[..., half:], a[..., :half]], -1)
    padk = lambda a: jnp.concatenate([a, jnp.zeros(a.shape[:-1] + (QK_NOPE - a.shape[-1],), a.dtype)], -1)
    w_kr = w_in[:, o_kr:o_g]
    w_rw = bf(rw_cols(w_in))
    w_mla = bf(jnp.concatenate([w_in[:, o_q:o_kr], padk(w_kr), padk(swap(w_kr))], -1))
    w_gate = bf(w_in[:, o_g:])
    g_mix = prm["norm_mix_g"][None]

    tm_all = _pick(l, (384, 128))
    tm_in = _pick(l, (1408, 384, 128))
    pr = _norm_mm(hp, g_mix, w_rw, tm_in, _pick(w_rw.shape[1], (512, 256, 128)), name="in_rwkv")
    pm = _norm_mm(hp, g_mix, w_mla, tm_in, _pick(w_mla.shape[1], (640, 128)), name="in_mla")
    pg = _norm_mm(hp, g_mix, w_gate, tm_in, 512, name="in_gate")

    zl = jnp.zeros((LORA, c), F32)
    w_lora = bf(jnp.concatenate([
        jnp.concatenate([prm["w_up"][0], zl, zl, zl], 1), jnp.concatenate([zl, prm["w_up"][1], zl, zl], 1),
        jnp.concatenate([zl, zl, prm["a_up"][0], zl], 1), jnp.concatenate([zl, zl, zl, prm["a_up"][1]], 1)], 0))
    g_up = bf(jnp.concatenate([prm["g_up"], jnp.zeros((2 * QK_NOPE - GATE_LORA, c), F32)], 0))
    red_one, spread = _segment_mats(c, RWKV_HEAD, 1.0)
    r, v, kk, lwf, lwb, ktf, ktb, bfw, bbw, g, bonus = _rwkv_prep(
        pr, rw_cols(prm["shift_c"]), w_lora, prm["w0"], prm["a0"], g_up, prm["k_k"][None], prm["k_a"][None],
        prm["r_k"].reshape(1, c), red_one, spread, seq, 128)
    t_i = np.arange(CHUNK)
    tri_f = jnp.asarray(t_i[:, None] >= t_i[None, :], BF16)
    tri_b = jnp.asarray(t_i[:, None] <= t_i[None, :], BF16)
    y_f, y_b = _scan_both(r, v, kk, lwf, ktf, bfw, lwb, ktb, bbw, tri_f, tri_b, seq)

    mem = np.arange(l)
    pos = np.where(mem < seq, mem + N_META, np.maximum(mem - seq - PAD_ROWS, 0)).astype(np.float32)
    inv_freq = ROPE_THETA ** (-jnp.arange(0, QK_ROPE, 2, dtype=F32) / QK_ROPE)
    ang = jnp.asarray(pos)[:, None] * inv_freq[None, :]
    cos, sin = jnp.cos(ang), jnp.sin(ang)
    z2 = jnp.zeros((l, QK_NOPE - QK_ROPE), F32)
    cs = jnp.concatenate([cos, cos, z2], 1)
    sn = jnp.concatenate([-sin, sin, z2], 1)
    qh = prm["w_uq"].reshape(q_lora, MLA_HEADS, QK_NOPE + QK_ROPE)
    heads = lambda a: a.reshape(a.shape[0], -1)
    wq = bf(jnp.concatenate([heads(qh[..., :QK_NOPE]), heads(padk(qh[..., QK_NOPE:])),
                             heads(padk(swap(qh[..., QK_NOPE:])))], -1))
    kvh = prm["w_ukv"].reshape(kv_lora, MLA_HEADS, QK_NOPE + V_HEAD)
    wkv = bf(jnp.concatenate([heads(kvh[..., :QK_NOPE]), heads(kvh[..., QK_NOPE:])], -1))
    tm_x = _pick(seq, (512, 256, 128))
    scale = (QK_NOPE + QK_ROPE) ** -0.5 * math.log2(math.e)
    q = _q_proj(pm, prm["q_norm_g"][None], wq, cs, sn, seq, tm_x, scale)
    k, vv = _kv_proj(pm, prm["kv_norm_g"][None], wkv, cs, sn, tm_all)
    bias = jnp.asarray(np.where(np.arange(TAIL) < PAD_ROWS, -1e30, 0.0)[None], F32)
    y_m = _attention(q, k, vv, bias, seq, _pick(seq, (1024, 512, 256, 128)))

    red_avg, _ = _segment_mats(c, RWKV_HEAD, 1.0 / RWKV_HEAD)
    merged = _merge(y_f, y_b, bonus, g, prm["ln_x_g"][None], prm["ln_x_b"][None], red_avg, spread, y_m, pg,
                    prm["b_gate"][None], bf(prm["p_rwkv"]), bf(prm["p_mla"]), seq, tm_x, 512)
    h2 = _out_proj(merged, bf(prm["w_o"]), hp, seq, tm_x, 512).reshape(b * seq, d)

    keys = bf(prm["peer_keys"].reshape(PEER_HEADS * 2, N_KEYS, -1))
    g_ffn = prm["norm_ffn_g"][None]
    experts, gates = _route(h2, g_ffn, bf(prm["peer_wq"]), keys, _pick(b * seq, (256, 128)))
    return experts, gates, h2, g_ffn


def kernel(x, meta_tokens, norm_mix_g, w_in, b_gate, shift_c, w_up, w0, a_up, a0, g_up, k_k, k_a, r_k,
           ln_x_g, ln_x_b, q_norm_g, w_uq, kv_norm_g, w_ukv, p_rwkv, p_mla, w_o, norm_ffn_g,
           peer_wq, peer_keys, peer_u, peer_v, final_norm_g):
    b, seq, d = x.shape
    assert norm_mix_g.shape[0] == 1, "single-layer block"
    assert seq % 128 == 0 and meta_tokens.shape[0] == N_META
    meta = jnp.broadcast_to(meta_tokens.astype(x.dtype)[None], (b, N_META, d))
    hp = jnp.concatenate([x, jnp.zeros((b, PAD_ROWS, d), x.dtype), meta], axis=1)
    prm = dict(norm_mix_g=norm_mix_g[0], w_in=w_in[0], b_gate=b_gate[0], shift_c=shift_c[0], w_up=w_up[0],
               w0=w0[0], a_up=a_up[0], a0=a0[0], g_up=g_up[0], k_k=k_k[0], k_a=k_a[0], r_k=r_k[0],
               ln_x_g=ln_x_g[0], ln_x_b=ln_x_b[0], q_norm_g=q_norm_g[0], w_uq=w_uq[0], kv_norm_g=kv_norm_g[0],
               w_ukv=w_ukv[0], p_rwkv=p_rwkv[0], p_mla=p_mla[0], w_o=w_o[0], norm_ffn_g=norm_ffn_g[0],
               peer_wq=peer_wq[0], peer_keys=peer_keys[0])
    table = _expert_table(peer_u[0], peer_v[0])
    experts, gates, h2, g_ffn = _layer(hp, seq, prm, table)
    out = _experts(experts, h2, gates, g_ffn, final_norm_g[None], table)
    return out.reshape(b, seq, d)
```
